```python
import math
import jax, jax.numpy as jnp
from jax import lax
import numpy as np

D_MODEL = 1024
BATCH = 16
SEQ = 2048
DEPTH = 2
DEC_BATCH = 8
DEC_SEQ = 32
PAST_LEN = 4096

CHUNK = 64
Q_BLOCK = 128
ML_CHUNK = CHUNK
ML_HEADS = 4
ML_DK = 128
ML_DV = 128
SC_W = 512
CONV_W = 3
DA_HEADS = 4
DA_DQK = 64
DA_DV = 2 * DA_DQK
BR_W = 512
N_BRANCH = 3
D_FF = 2816
EPS = 1e-6
SPLIT_SIZES = (ML_HEADS * ML_DK, ML_HEADS * ML_DK, ML_HEADS * ML_DV, ML_HEADS * ML_DV, ML_HEADS, ML_HEADS,
               SC_W, SC_W, SC_W,
               DA_HEADS * 2 * DA_DQK, DA_HEADS * 2 * DA_DQK, DA_HEADS * DA_DV,
               N_BRANCH * D_MODEL)
IN_WIDTH = sum(SPLIT_SIZES)

kernel_name = "hybrid_mlstm_shortconv_diffattn_stream_step"


def rmsnorm(x, g):
    xf = x.astype(jnp.float32)
    y = xf * lax.rsqrt(jnp.mean(xf * xf, axis=-1, keepdims=True) + EPS)
    return (y * g.astype(jnp.float32)).astype(x.dtype)


def causal_dwconv(x, hist, w):
    T = x.shape[1]
    xp = jnp.concatenate([hist.astype(x.dtype), x], axis=1)
    y = sum(w[j] * xp[:, j:j + T] for j in range(CONV_W))
    return y, xp[:, -(CONV_W - 1):]


def mlstm_chunk(carry, inp):
    C, n, m = carry
    q, k, v, ig, lf = inp
    L = q.shape[1]
    b = jnp.cumsum(lf, axis=1).transpose(0, 2, 1)
    igh = ig.transpose(0, 2, 1)
    causal = jnp.tril(jnp.ones((L, L), dtype=bool))
    dlog = jnp.where(causal, b[..., :, None] - b[..., None, :] + igh[..., None, :], -jnp.inf)
    a = b + m[..., None]
    mt = jnp.maximum(a, jnp.max(dlog, axis=-1))
    w = jnp.exp(dlog - mt[..., None])
    inter = jnp.exp(a - mt)
    s = jnp.einsum('blhd,bshd->bhls', q, k) * w
    num = (jnp.einsum('bhls,bshe->blhe', s, v)
           + inter.transpose(0, 2, 1)[..., None] * jnp.einsum('blhd,bhde->blhe', q, C))
    den = jnp.sum(s, axis=-1) + inter * jnp.einsum('blhd,bhd->bhl', q, n)
    den = jnp.maximum(jnp.abs(den), jnp.exp(-mt))
    h = num / den.transpose(0, 2, 1)[..., None]
    wl = w[..., -1, :]
    il = inter[..., -1]
    C_new = il[..., None, None] * C + jnp.einsum('bhs,bshd,bshe->bhde', wl, k, v)
    n_new = il[..., None] * n + jnp.einsum('bhs,bshd->bhd', wl, k)
    return (C_new, n_new, mt[..., -1]), h


def mlstm_prompt(q, k, v, ig, lf):
    Bsz, S = q.shape[:2]
    nc = S // ML_CHUNK

    def to_chunks(t):
        return jnp.moveaxis(t.reshape((Bsz, nc, ML_CHUNK) + t.shape[2:]), 1, 0)

    init = (jnp.zeros((Bsz, ML_HEADS, ML_DK, ML_DV), jnp.float32),
            jnp.zeros((Bsz, ML_HEADS, ML_DK), jnp.float32),
            jnp.zeros((Bsz, ML_HEADS), jnp.float32))
    state, h = lax.scan(mlstm_chunk, init, (to_chunks(q), to_chunks(k), to_chunks(v), to_chunks(ig), to_chunks(lf)))
    return state, jnp.moveaxis(h, 0, 1).reshape(Bsz, S, ML_HEADS, ML_DV)


def diff_attend(q, k, v, lam, mask):
    s = jnp.einsum('bqhcd,bkhcd->bhcqk', q, k).astype(jnp.float32)
    if mask is not None:
        s = jnp.where(mask, s, -jnp.inf)
    p = jax.nn.softmax(s, axis=-1)
    wdiff = p[:, :, 0] - lam * p[:, :, 1]
    return jnp.einsum('bhqk,bkhe->bqhe', wdiff.astype(v.dtype), v)


def diff_attn_prompt(q, k, v, lam):
    Bsz, S = q.shape[:2]
    nb = S // Q_BLOCK
    qb = jnp.moveaxis(q.reshape((Bsz, nb, Q_BLOCK) + q.shape[2:]), 1, 0)
    qpos = jnp.arange(S).reshape(nb, Q_BLOCK)
    kchunk = jnp.arange(S) // CHUNK

    def one(args):
        qblk, pos = args
        mask = kchunk[None, :] <= (pos // CHUNK)[:, None]
        return diff_attend(qblk, k, v, lam, mask)

    o = lax.map(one, (qb, qpos))
    return jnp.moveaxis(o, 0, 1).reshape(Bsz, S, DA_HEADS, DA_DV)


def layer(x, l, w, past):
    (g1, w_in, b_if, ml_g, sc_w, lam_p, da_g, w_br, w_o, g2, w_up, f_w, w_dn) = w
    Bsz, T, _ = x.shape
    f32 = jnp.float32
    h = rmsnorm(x, g1)
    z = h @ w_in
    splits = tuple(int(i) for i in np.cumsum(SPLIT_SIZES)[:-1])
    (mq, mk, mv, mo, mi, mf, sb, sc, sx, dq, dk, dv, gt) = jnp.split(z, splits, axis=-1)

    q = mq.reshape(Bsz, T, ML_HEADS, ML_DK).astype(f32)
    k = mk.reshape(Bsz, T, ML_HEADS, ML_DK).astype(f32) * (ML_DK ** -0.5)
    v = mv.reshape(Bsz, T, ML_HEADS, ML_DV).astype(f32)
    ig = mi.astype(f32) + b_if[0].astype(f32)
    lf = jax.nn.log_sigmoid(mf.astype(f32) + b_if[1].astype(f32))
    if past is None:
        (C, n, m), hm = mlstm_prompt(q, k, v, ig, lf)
    else:
        carry = (past[2].astype(f32), past[3].astype(f32), past[4].astype(f32))
        (C, n, m), hm = mlstm_chunk(carry, (q, k, v, ig, lf))
    hm = rmsnorm(hm, ml_g).astype(x.dtype).reshape(Bsz, T, ML_HEADS * ML_DV)
    ya = jax.nn.sigmoid(mo) * hm

    hist = jnp.zeros((Bsz, CONV_W - 1, SC_W), x.dtype) if past is None else past[5]
    cu, conv_state = causal_dwconv(sc * sx, hist, sc_w)
    yb = sb * cu

    lam_init = 0.8 - 0.6 * math.exp(-0.3 * l)
    lp = lam_p.astype(f32)
    lam = jnp.exp(jnp.sum(lp[0] * lp[1])) - jnp.exp(jnp.sum(lp[2] * lp[3])) + lam_init
    q5 = dq.reshape(Bsz, T, DA_HEADS, 2, DA_DQK) * (DA_DQK ** -0.5)
    kk = dk.reshape(Bsz, T, DA_HEADS, 2 * DA_DQK)
    vv = dv.reshape(Bsz, T, DA_HEADS, DA_DV)
    if past is None:
        o = diff_attn_prompt(q5, kk.reshape(Bsz, T, DA_HEADS, 2, DA_DQK), vv, lam)
    else:
        keys = jnp.concatenate([past[0].astype(x.dtype), kk], axis=1)
        vals = jnp.concatenate([past[1].astype(x.dtype), vv], axis=1)
        keys = keys.reshape(Bsz, keys.shape[1], DA_HEADS, 2, DA_DQK)
        o = diff_attend(q5, keys, vals, lam, None)
    o = rmsnorm(o, da_g) * (1.0 - lam_init)
    yc = o.reshape(Bsz, T, DA_HEADS * DA_DV)

    br = jnp.stack([ya, yb, yc], axis=2)
    proj = jnp.einsum('btnc,ncd->btnd', br, w_br)
    gates = jax.nn.sigmoid(gt.reshape(Bsz, T, N_BRANCH, D_MODEL))
    x = x + jnp.einsum('btnd,btnd->btd', gates, proj) @ w_o

    h2 = rmsnorm(x, g2)
    up = h2 @ w_up
    fhist = jnp.zeros((Bsz, CONV_W - 1, 2 * D_FF), x.dtype) if past is None else past[6]
    upc, ffn_state = causal_dwconv(up, fhist, f_w)
    x = x + (jax.nn.silu(upc[..., :D_FF]) * upc[..., D_FF:]) @ w_dn
    new_state = (kk, vv, C.astype(x.dtype), n.astype(x.dtype), m.astype(x.dtype),
                 conv_state, ffn_state.astype(x.dtype))
    return x, new_state


def setup_inputs(seed: int = 0) -> dict:
    key = jax.random.key(seed)
    ks = jax.random.split(key, 24)
    nrm = jax.random.normal
    f = jnp.float32
    return {
        "x_prompt": nrm(ks[0], (BATCH, SEQ, D_MODEL), f),
        "x_sample": nrm(ks[1], (DEC_BATCH, DEC_SEQ, D_MODEL), f),
        "cache_k": nrm(ks[2], (DEPTH, DEC_BATCH, PAST_LEN, DA_HEADS, 2 * DA_DQK), f),
        "cache_v": nrm(ks[3], (DEPTH, DEC_BATCH, PAST_LEN, DA_HEADS, DA_DV), f),
        "state_C": 0.1 * nrm(ks[4], (DEPTH, DEC_BATCH, ML_HEADS, ML_DK, ML_DV), f),
        "state_n": 0.1 * nrm(ks[5], (DEPTH, DEC_BATCH, ML_HEADS, ML_DK), f),
        "state_m": 0.5 * nrm(ks[6], (DEPTH, DEC_BATCH, ML_HEADS), f),
        "state_conv": nrm(ks[7], (DEPTH, DEC_BATCH, CONV_W - 1, SC_W), f),
        "state_ffn": nrm(ks[8], (DEPTH, DEC_BATCH, CONV_W - 1, 2 * D_FF), f),
        "norm1": 1.0 + 0.02 * nrm(ks[9], (DEPTH, D_MODEL), f),
        "w_in": nrm(ks[10], (DEPTH, D_MODEL, IN_WIDTH), f) * D_MODEL ** -0.5,
        "b_if": jnp.stack([0.1 * nrm(ks[11], (DEPTH, ML_HEADS), f),
                           3.0 + 0.1 * nrm(ks[12], (DEPTH, ML_HEADS), f)], axis=1),
        "ml_norm": 1.0 + 0.02 * nrm(ks[13], (DEPTH, ML_HEADS, ML_DV), f),
        "w_sc_conv": nrm(ks[14], (DEPTH, CONV_W, SC_W), f) * CONV_W ** -0.5,
        "da_lambda": 0.1 * nrm(ks[15], (DEPTH, 4, DA_DQK), f),
        "da_norm": 1.0 + 0.02 * nrm(ks[16], (DEPTH, DA_DV), f),
        "w_branch": nrm(ks[17], (DEPTH, N_BRANCH, BR_W, D_MODEL), f) * BR_W ** -0.5,
        "w_out": nrm(ks[18], (DEPTH, D_MODEL, D_MODEL), f) * D_MODEL ** -0.5,
        "norm2": 1.0 + 0.02 * nrm(ks[19], (DEPTH, D_MODEL), f),
        "w_up": nrm(ks[20], (DEPTH, D_MODEL, 2 * D_FF), f) * D_MODEL ** -0.5,
        "w_fconv": nrm(ks[21], (DEPTH, CONV_W, 2 * D_FF), f) * CONV_W ** -0.5,
        "w_down": nrm(ks[22], (DEPTH, D_FF, D_MODEL), f) * D_FF ** -0.5,
        "norm_f": 1.0 + 0.02 * nrm(ks[23], (D_MODEL,), f),
    }


def reference(x_prompt, x_sample, cache_k, cache_v, state_C, state_n, state_m, state_conv, state_ffn,
              norm1, w_in, b_if, ml_norm, w_sc_conv, da_lambda, da_norm, w_branch, w_out,
              norm2, w_up, w_fconv, w_down, norm_f):
    xp = x_prompt
    xs = x_sample
    prompt_states = []
    sample_states = []
    for l in range(DEPTH):
        w = (norm1[l], w_in[l], b_if[l], ml_norm[l], w_sc_conv[l], da_lambda[l], da_norm[l],
             w_branch[l], w_out[l], norm2[l], w_up[l], w_fconv[l], w_down[l])
        xp, st_p = layer(xp, l, w, None)
        past = (cache_k[l], cache_v[l], state_C[l], state_n[l], state_m[l], state_conv[l], state_ffn[l])
        xs, st_s = layer(xs, l, w, past)
        prompt_states.append(st_p)
        sample_states.append(st_s)
    y_prompt = rmsnorm(xp, norm_f)
    y_sample = rmsnorm(xs, norm_f)
    p_k, p_v, p_C, p_n, p_m, p_conv, p_ffn = [jnp.stack(t) for t in zip(*prompt_states)]
    s_k, s_v, s_C, s_n, s_m, s_conv, s_ffn = [jnp.stack(t) for t in zip(*sample_states)]
    return (y_prompt, y_sample, p_k, p_v, p_C, p_n, p_m, p_conv, p_ffn,
            s_k, s_v, s_C, s_n, s_m, s_conv, s_ffn)
```

```python
import functools
import math

import jax
import jax.numpy as jnp
from jax import lax
from jax.experimental import pallas as pl
from jax.experimental.pallas import tpu as pltpu

F32 = jnp.float32
BF16 = jnp.bfloat16

EPS = 1e-6
ML_HEADS = 4
ML_D = 128
SC_W = 512
DA_HEADS = 4
DA_DQK = 64
DA_DV = 128
ATT_CHUNK = 64
CONV_W = 3
HIST = CONV_W - 1
HIST_ROW = 8 - HIST
GATE_W = 2 * ML_HEADS
LANES = 128
HEAD_W = ML_HEADS * ML_D
Z_W = 10 * HEAD_W + 3 * 1024
LN_K_SCALE = -0.5 * math.log(ML_D)
VMEM_LIMIT = 56 * 2 ** 20

Z_MQ, Z_MK, Z_MV, Z_MO, Z_SB, Z_SC, Z_SX, Z_DQ, Z_DK, Z_DV = range(10)
Z_GT = 5


def _params(n_axes):
    return pltpu.CompilerParams(dimension_semantics=("arbitrary",) * n_axes,
                                vmem_limit_bytes=VMEM_LIMIT)


def _resident(block_shape, index_map):
    return pl.BlockSpec(block_shape, index_map, pipeline_mode=pl.Buffered(1))


def _rms(x, g):
    ms = jnp.mean(x * x, axis=-1, keepdims=True)
    return x * lax.rsqrt(ms + EPS) * g


def _dot(a, b):
    return jnp.dot(a, b, preferred_element_type=F32)


def _dot_nt(a, b):
    return lax.dot_general(a, b, (((1,), (1,)), ((), ())), preferred_element_type=F32)


def _dot_tn(a, b):
    return lax.dot_general(a, b, (((0,), (0,)), ((), ())), preferred_element_type=F32)


def _inproj_kernel(x_ref, g_ref, w_ref, wg_ref, wgt_ref, z_ref, k_ref, v_ref, gate_ref, gatet_ref):
    h = _rms(x_ref[...], g_ref[...]).astype(BF16)
    for c in range(Z_W // HEAD_W):
        cols = slice(c * HEAD_W, (c + 1) * HEAD_W)
        zc = _dot(h, w_ref[:, cols])
        z_ref[:, cols] = zc.astype(BF16)
        if c == Z_DK:
            k_ref[...] = zc
        if c == Z_DV:
            v_ref[...] = zc
    gate_ref[...] = _dot(h, wg_ref[...])
    gatet_ref[...] = _dot_nt(wgt_ref[...], h)[:GATE_W, :]


def _in_proj(x2d, norm1, w_main, w_gate, w_gate_t, layer, tm):
    n, d = x2d.shape
    row = lambda i: (i, 0)
    lay3 = lambda i: (layer, 0, 0)
    return pl.pallas_call(
        _inproj_kernel,
        grid=(n // tm,),
        in_specs=[
            pl.BlockSpec((tm, d), row),
            pl.BlockSpec((None, 1, d), lay3),
            _resident((None, d, Z_W), lay3),
            _resident((None, d, LANES), lay3),
            _resident((None, LANES, d), lay3),
        ],
        out_specs=[
            pl.BlockSpec((tm, Z_W), row),
            pl.BlockSpec((tm, HEAD_W), row),
            pl.BlockSpec((tm, HEAD_W), row),
            pl.BlockSpec((tm, LANES), row),
            pl.BlockSpec((GATE_W, tm), lambda i: (0, i)),
        ],
        out_shape=[
            jax.ShapeDtypeStruct((n, Z_W), BF16),
            jax.ShapeDtypeStruct((n, HEAD_W), F32),
            jax.ShapeDtypeStruct((n, HEAD_W), F32),
            jax.ShapeDtypeStruct((n, LANES), F32),
            jax.ShapeDtypeStruct((GATE_W, n), F32),
        ],
        compiler_params=_params(1),
        name="in_proj",
    )(x2d, norm1, w_main, w_gate, w_gate_t)


def _log_sigmoid(x):
    return jnp.minimum(x, 0.0) - jnp.log1p(jnp.exp(-jnp.abs(x)))


def _split_bf16(x):
    hi = x.astype(BF16)
    lo = (x - hi.astype(F32)).astype(BF16)
    return hi, lo


def _mlstm_kernel(q_ref, k_ref, v_ref, o_ref, gate_ref, gatet_ref, brow_ref, bcol_ref, mlg_ref,
                  c0_ref, n0_ref, m0_ref, ya_ref, c_ref, n_ref, m_ref, *, length):
    L = length

    @pl.when(pl.program_id(1) == 0)
    def _():
        c_ref[...] = c0_ref[...]
        n_ref[...] = n0_ref[...]
        m_ref[...] = m0_ref[...]

    rows = lax.broadcasted_iota(jnp.int32, (L, L), 0)
    cols = lax.broadcasted_iota(jnp.int32, (L, L), 1)
    causal = rows >= cols
    tri = jnp.where(causal, 1.0, 0.0).astype(BF16)

    pre_c = gate_ref[...] + brow_ref[...]
    pre_r = gatet_ref[...] + bcol_ref[...]
    hi, lo = _split_bf16(_log_sigmoid(pre_c))
    cum_c = _dot(tri, hi) + _dot(tri, lo)
    hi, lo = _split_bf16(_log_sigmoid(pre_r))
    cum_r = _dot_nt(hi, tri) + _dot_nt(lo, tri)

    for h in range(ML_HEADS):
        hs = slice(h * ML_D, (h + 1) * ML_D)
        q = q_ref[:, hs]
        k = k_ref[:, hs]
        v = v_ref[:, hs]
        b_col = cum_c[:, ML_HEADS + h:ML_HEADS + h + 1]
        c_col = pre_c[:, h:h + 1] - b_col
        c_row = pre_r[h:h + 1, :] - cum_r[ML_HEADS + h:ML_HEADS + h + 1, :]
        m_prev = m_ref[0, h:h + 1, 0:1]
        c_prev = c_ref[0, h]
        n_prev = n_ref[0, h:h + 1, :]

        d = jnp.where(causal, c_row, -jnp.inf)
        mx = jnp.maximum(jnp.max(d, axis=1, keepdims=True), m_prev)
        w = jnp.exp(d + LN_K_SCALE - mx)
        inter = jnp.exp(m_prev - mx)
        s = _dot_nt(q, k) * w
        num = _dot(s.astype(BF16), v) + inter * _dot(q, c_prev.astype(BF16))
        qn = jnp.sum(q.astype(F32) * n_prev, axis=1, keepdims=True)
        den = jnp.sum(s, axis=1, keepdims=True) + inter * qn
        den = jnp.maximum(jnp.abs(den), jnp.exp(-(b_col + mx)))
        hn = _rms(num / den, mlg_ref[h:h + 1, :])
        ya = jax.nn.sigmoid(o_ref[:, hs].astype(F32)) * hn
        ya_ref[:, hs] = ya.astype(BF16)

        mx_last = mx[L - 1:L, :]
        wl = jnp.exp(c_col + LN_K_SCALE - mx_last)
        il = jnp.exp(m_prev - mx_last)
        kw = k.astype(F32) * wl
        c_ref[0, h] = il * c_prev + _dot_tn(kw.astype(BF16), v)
        n_ref[0, h:h + 1, :] = il * n_prev + jnp.sum(kw, axis=0, keepdims=True)
        m_ref[0, h:h + 1, :] = jnp.broadcast_to(b_col[L - 1:L, :] + mx_last, (1, LANES))


def _mlstm(z, gate, gate_t, bias_row, bias_col, ml_g, c0, n0, m0, layer, bsz, seq, length):
    n = bsz * seq
    nc = seq // length
    row = lambda col: (lambda b, j: (b * nc + j, col))
    lay3 = lambda b, j: (layer, 0, 0)
    state4 = lambda b, j: (b, 0, 0, 0)
    state3 = lambda b, j: (b, 0, 0)
    return pl.pallas_call(
        functools.partial(_mlstm_kernel, length=length),
        grid=(bsz, nc),
        in_specs=[
            pl.BlockSpec((length, HEAD_W), row(Z_MQ)),
            pl.BlockSpec((length, HEAD_W), row(Z_MK)),
            pl.BlockSpec((length, HEAD_W), row(Z_MV)),
            pl.BlockSpec((length, HEAD_W), row(Z_MO)),
            pl.BlockSpec((length, LANES), row(0)),
            pl.BlockSpec((None, GATE_W, length), lambda b, j: (b * nc + j, 0, 0)),
            pl.BlockSpec((None, 1, LANES), lay3),
            pl.BlockSpec((None, GATE_W, 1), lay3),
            pl.BlockSpec((None, ML_HEADS, ML_D), lay3),
            pl.BlockSpec((1, ML_HEADS, ML_D, ML_D), state4),
            pl.BlockSpec((1, ML_HEADS, ML_D), state3),
            pl.BlockSpec((1, ML_HEADS, LANES), state3),
        ],
        out_specs=[
            pl.BlockSpec((length, HEAD_W), row(0)),
            pl.BlockSpec((1, ML_HEADS, ML_D, ML_D), state4),
            pl.BlockSpec((1, ML_HEADS, ML_D), state3),
            pl.BlockSpec((1, ML_HEADS, LANES), state3),
        ],
        out_shape=[
            jax.ShapeDtypeStruct((n, HEAD_W), BF16),
            jax.ShapeDtypeStruct((bsz, ML_HEADS, ML_D, ML_D), F32),
            jax.ShapeDtypeStruct((bsz, ML_HEADS, ML_D), F32),
            jax.ShapeDtypeStruct((bsz, ML_HEADS, LANES), F32),
        ],
        compiler_params=_params(2),
        name="mlstm",
    )(z, z, z, z, gate, gate_t, bias_row, bias_col, ml_g, c0, n0, m0)


def _lambda(lam_ref, lam_init):
    lp = lam_ref[...]
    a = jnp.sum(lp[0:1] * lp[1:2], axis=1, keepdims=True)
    b = jnp.sum(lp[2:3] * lp[3:4], axis=1, keepdims=True)
    return jnp.exp(a) - jnp.exp(b) + lam_init


def _split_q(q):
    qs = q * (DA_DQK ** -0.5)
    lane = lax.broadcasted_iota(jnp.int32, q.shape, 1)
    zero = jnp.zeros_like(qs)
    return jnp.where(lane < DA_DQK, qs, zero), jnp.where(lane >= DA_DQK, qs, zero)


def _attn_finish(acc0, l0, acc1, l1, lam, dag, lam_init):
    o = acc0 / l0 - lam * (acc1 / l1)
    return _rms(o, dag) * (1.0 - lam_init)


def _attn_prompt_kernel(lam_ref, dag_ref, q_ref, k_ref, v_ref, o_ref, *, blk, lam_init):
    i = pl.program_id(2)
    q0, q1 = _split_q(q_ref[...])

    def update(carry, kb, vb, mask):
        out = []
        for qc, (m, l, acc) in zip((q0, q1), carry):
            s = _dot_nt(qc, kb)
            if mask is not None:
                s = jnp.where(mask, s, -jnp.inf)
            m_new = jnp.maximum(m, jnp.max(s, axis=1, keepdims=True))
            p = jnp.exp(s - m_new)
            alpha = jnp.exp(m - m_new)
            l_new = alpha * l + jnp.sum(p, axis=1, keepdims=True)
            acc_new = alpha * acc + _dot(p.astype(BF16), vb)
            out.append((m_new, l_new, acc_new))
        return tuple(out)

    def body(j, carry):
        start = pl.multiple_of(j * blk, blk)
        return update(carry, k_ref[pl.ds(start, blk), :], v_ref[pl.ds(start, blk), :], None)

    init1 = (jnp.full((blk, 1), -jnp.inf, F32), jnp.zeros((blk, 1), F32), jnp.zeros((blk, DA_DV), F32))
    carry = lax.fori_loop(0, i, body, (init1, init1))

    rows = lax.broadcasted_iota(jnp.int32, (blk, blk), 0)
    cols = lax.broadcasted_iota(jnp.int32, (blk, blk), 1)
    mask = (cols // ATT_CHUNK) <= (rows // ATT_CHUNK)
    start = pl.multiple_of(i * blk, blk)
    (_, l0, acc0), (_, l1, acc1) = update(
        carry, k_ref[pl.ds(start, blk), :], v_ref[pl.ds(start, blk), :], mask)
    out = _attn_finish(acc0, l0, acc1, l1, _lambda(lam_ref, lam_init), dag_ref[...], lam_init)
    o_ref[...] = out.astype(BF16)


def _attn_prompt(z, da_lambda, da_norm, layer, bsz, seq, blk):
    nq = seq // blk
    lam_init = 0.8 - 0.6 * math.exp(-0.3 * layer)
    lay3 = lambda b, h, i: (layer, 0, 0)
    return pl.pallas_call(
        functools.partial(_attn_prompt_kernel, blk=blk, lam_init=lam_init),
        grid=(bsz, DA_HEADS, nq),
        in_specs=[
            pl.BlockSpec((None, 4, DA_DQK), lay3),
            pl.BlockSpec((None, 1, DA_DV), lay3),
            pl.BlockSpec((blk, DA_DV), lambda b, h, i: (b * nq + i, Z_DQ * DA_HEADS + h)),
            pl.BlockSpec((seq, DA_DV), lambda b, h, i: (b, Z_DK * DA_HEADS + h)),
            pl.BlockSpec((seq, DA_DV), lambda b, h, i: (b, Z_DV * DA_HEADS + h)),
        ],
        out_specs=pl.BlockSpec((blk, DA_DV), lambda b, h, i: (b * nq + i, h)),
        out_shape=jax.ShapeDtypeStruct((bsz * seq, HEAD_W), BF16),
        compiler_params=_params(3),
        name="attn_prompt",
    )(da_lambda, da_norm, z, z, z)


def _attn_sample_kernel(lam_ref, dag_ref, q_ref, kn_ref, vn_ref, kp_ref, vp_ref, o_ref, *, lam_init):
    q0, q1 = _split_q(q_ref[...])
    kp = kp_ref[...].astype(BF16)
    vp = vp_ref[...].astype(BF16)
    kn = kn_ref[...]
    vn = vn_ref[...]
    res = []
    for qc in (q0, q1):
        sp = _dot_nt(qc, kp)
        sn = _dot_nt(qc, kn)
        m = jnp.maximum(jnp.max(sp, axis=1, keepdims=True), jnp.max(sn, axis=1, keepdims=True))
        pp = jnp.exp(sp - m)
        pn = jnp.exp(sn - m)
        l = jnp.sum(pp, axis=1, keepdims=True) + jnp.sum(pn, axis=1, keepdims=True)
        acc = _dot(pp.astype(BF16), vp) + _dot(pn.astype(BF16), vn)
        res.append((acc, l))
    (acc0, l0), (acc1, l1) = res
    out = _attn_finish(acc0, l0, acc1, l1, _lambda(lam_ref, lam_init), dag_ref[...], lam_init)
    o_ref[...] = out.astype(BF16)


def _attn_sample(z, cache_k, cache_v, da_lambda, da_norm, layer, bsz, seq):
    past = cache_k.shape[2]
    lam_init = 0.8 - 0.6 * math.exp(-0.3 * layer)
    lay3 = lambda b, h: (layer, 0, 0)
    cache = lambda b, h: (layer, b, 0, h)
    return pl.pallas_call(
        functools.partial(_attn_sample_kernel, lam_init=lam_init),
        grid=(bsz, DA_HEADS),
        in_specs=[
            pl.BlockSpec((None, 4, DA_DQK), lay3),
            pl.BlockSpec((None, 1, DA_DV), lay3),
            pl.BlockSpec((seq, DA_DV), lambda b, h: (b, Z_DQ * DA_HEADS + h)),
            pl.BlockSpec((seq, DA_DV), lambda b, h: (b, Z_DK * DA_HEADS + h)),
            pl.BlockSpec((seq, DA_DV), lambda b, h: (b, Z_DV * DA_HEADS + h)),
            pl.BlockSpec((None, None, past, DA_DV), cache),
            pl.BlockSpec((None, None, past, DA_DV), cache),
        ],
        out_specs=pl.BlockSpec((seq, DA_DV), lambda b, h: (b, h)),
        out_shape=jax.ShapeDtypeStruct((bsz * seq, HEAD_W), BF16),
        compiler_params=_params(2),
        name="attn_sample",
    )(da_lambda, da_norm, z, z, z, cache_k, cache_v)


def _load_history(buf_ref, hist_ref, tm):
    @pl.when(pl.program_id(1) == 0)
    def _():
        buf_ref[HIST_ROW:8, :] = hist_ref[0]

    @pl.when(pl.program_id(1) > 0)
    def _():
        buf_ref[HIST_ROW:8, :] = buf_ref[tm + HIST_ROW:tm + 8, :]


def _merge_kernel(x_ref, sb_ref, sc_ref, sx_ref, g0_ref, g1_ref, g2_ref, ya_ref, yc_ref, hist_ref,
                  scw_ref, wbr_ref, wo_ref, x1_ref, cs_ref, ubuf, *, tm):
    _load_history(ubuf, hist_ref, tm)
    u = sc_ref[...].astype(F32) * sx_ref[...].astype(F32)
    ubuf[8:8 + tm, :] = u
    cu = (scw_ref[2:3, :] * u + scw_ref[1:2, :] * ubuf[7:7 + tm, :]
          + scw_ref[0:1, :] * ubuf[6:6 + tm, :])
    cs_ref[0] = ubuf[tm + HIST_ROW:tm + 8, :]
    yb = (sb_ref[...].astype(F32) * cu).astype(BF16)
    mix = (jax.nn.sigmoid(g0_ref[...].astype(F32)) * _dot(ya_ref[...], wbr_ref[0])
           + jax.nn.sigmoid(g1_ref[...].astype(F32)) * _dot(yb, wbr_ref[1])
           + jax.nn.sigmoid(g2_ref[...].astype(F32)) * _dot(yc_ref[...], wbr_ref[2]))
    x1_ref[...] = x_ref[...] + _dot(mix.astype(BF16), wo_ref[...])


def _merge(x2d, z, ya, yc, hist, sc_w, w_br, w_o, layer, bsz, seq, tm):
    n, d = x2d.shape
    nt = seq // tm
    row = lambda col: (lambda b, j: (b * nt + j, col))
    lay3 = lambda b, j: (layer, 0, 0)
    state = lambda b, j: (b, 0, 0)
    return pl.pallas_call(
        functools.partial(_merge_kernel, tm=tm),
        grid=(bsz, nt),
        in_specs=[
            pl.BlockSpec((tm, d), row(0)),
            pl.BlockSpec((tm, HEAD_W), row(Z_SB)),
            pl.BlockSpec((tm, HEAD_W), row(Z_SC)),
            pl.BlockSpec((tm, HEAD_W), row(Z_SX)),
            pl.BlockSpec((tm, d), row(Z_GT)),
            pl.BlockSpec((tm, d), row(Z_GT + 1)),
            pl.BlockSpec((tm, d), row(Z_GT + 2)),
            pl.BlockSpec((tm, HEAD_W), row(0)),
            pl.BlockSpec((tm, HEAD_W), row(0)),
            pl.BlockSpec((1, HIST, SC_W), state),
            pl.BlockSpec((None, CONV_W, SC_W), lay3),
            _resident((None, 3, HEAD_W, d), lambda b, j: (layer, 0, 0, 0)),
            _resident((None, d, d), lay3),
        ],
        out_specs=[
            pl.BlockSpec((tm, d), row(0)),
            pl.BlockSpec((1, HIST, SC_W), state),
        ],
        out_shape=[
            jax.ShapeDtypeStruct((n, d), F32),
            jax.ShapeDtypeStruct((bsz, HIST, SC_W), F32),
        ],
        scratch_shapes=[pltpu.VMEM((tm + 8, SC_W), F32)],
        compiler_params=_params(2),
        name="merge",
    )(x2d, z, z, z, z, z, z, ya, yc, hist, sc_w, w_br, w_o)


def _ffn_kernel(x_ref, g2_ref, gf_ref, hist_ref, fw_ref, wup_ref, wdn_ref, y_ref, fs_ref, upbuf,
                *, tm, d_ff, chunk, final_norm):
    _load_history(upbuf, hist_ref, tm)
    x = x_ref[...]
    h2 = _rms(x, g2_ref[...]).astype(BF16)

    def conv_up(cols):
        up = _dot(h2, wup_ref[:, cols])
        upbuf[8:8 + tm, cols] = up
        return (fw_ref[2:3, cols] * up + fw_ref[1:2, cols] * upbuf[7:7 + tm, cols]
                + fw_ref[0:1, cols] * upbuf[6:6 + tm, cols])

    acc = jnp.zeros_like(x)
    for c in range(0, d_ff, chunk):
        a = conv_up(slice(c, c + chunk))
        b = conv_up(slice(d_ff + c, d_ff + c + chunk))
        act = (a * jax.nn.sigmoid(a) * b).astype(BF16)
        acc = acc + _dot(act, wdn_ref[c:c + chunk, :])
    fs_ref[0] = upbuf[tm + HIST_ROW:tm + 8, :]
    y = x + acc
    if final_norm:
        y = _rms(y, gf_ref[...])
    y_ref[...] = y


def _ffn(x2d, norm2, norm_f, hist, w_fconv, w_up, w_down, layer, bsz, seq, tm, final_norm):
    n, d = x2d.shape
    d_ff = w_down.shape[1]
    nt = seq // tm
    chunk = 256 if d_ff % 256 == 0 else LANES
    row = lambda b, j: (b * nt + j, 0)
    lay3 = lambda b, j: (layer, 0, 0)
    state = lambda b, j: (b, 0, 0)
    return pl.pallas_call(
        functools.partial(_ffn_kernel, tm=tm, d_ff=d_ff, chunk=chunk, final_norm=final_norm),
        grid=(bsz, nt),
        in_specs=[
            pl.BlockSpec((tm, d), row),
            pl.BlockSpec((None, 1, d), lay3),
            pl.BlockSpec((1, d), lambda b, j: (0, 0)),
            pl.BlockSpec((1, HIST, 2 * d_ff), state),
            pl.BlockSpec((None, CONV_W, 2 * d_ff), lay3),
            _resident((None, d, 2 * d_ff), lay3),
            _resident((None, d_ff, d), lay3),
        ],
        out_specs=[
            pl.BlockSpec((tm, d), row),
            pl.BlockSpec((1, HIST, 2 * d_ff), state),
        ],
        out_shape=[
            jax.ShapeDtypeStruct((n, d), F32),
            jax.ShapeDtypeStruct((bsz, HIST, 2 * d_ff), F32),
        ],
        scratch_shapes=[pltpu.VMEM((tm + 8, 2 * d_ff), F32)],
        compiler_params=_params(2),
        name="ffn",
    )(x2d, norm2, norm_f, hist, w_fconv, w_up, w_down)


def _tile(seq, target):
    return target if seq % target == 0 else seq


def _group_layer(x2d, layer, bsz, seq, wts, past, final_norm):
    d = x2d.shape[1]
    z, kk, vv, gate, gate_t = _in_proj(x2d, wts["norm1"], wts["w_main"], wts["w_gate"],
                                       wts["w_gate_t"], layer, _tile(bsz * seq, 512))
    if past is None:
        c0 = jnp.zeros((bsz, ML_HEADS, ML_D, ML_D), F32)
        n0 = jnp.zeros((bsz, ML_HEADS, ML_D), F32)
        m0 = jnp.zeros((bsz, ML_HEADS, LANES), F32)
        conv0 = jnp.zeros((bsz, HIST, SC_W), F32)
        ffn0 = jnp.zeros((bsz, HIST, wts["w_fconv"].shape[2]), F32)
    else:
        c0, n0 = past["C"][layer], past["n"][layer]
        m0 = jnp.broadcast_to(past["m"][layer][..., None], (bsz, ML_HEADS, LANES))
        conv0, ffn0 = past["conv"][layer], past["ffn"][layer]
    length = _tile(seq, 256)
    gate_t = gate_t.reshape(GATE_W, bsz * seq // length, length).swapaxes(0, 1)
    ya, c_new, n_new, m_new = _mlstm(z, gate, gate_t, wts["bias_row"], wts["bias_col"], wts["ml_norm"],
                                     c0, n0, m0, layer, bsz, seq, length)
    if past is None:
        yc = _attn_prompt(z, wts["da_lambda"], wts["da_norm"], layer, bsz, seq, 256)
    else:
        yc = _attn_sample(z, past["k"], past["v"], wts["da_lambda"], wts["da_norm"], layer, bsz, seq)
    x1, conv_new = _merge(x2d, z, ya, yc, conv0, wts["w_sc_conv"], wts["w_branch"], wts["w_out"],
                          layer, bsz, seq, _tile(seq, 512))
    x2, ffn_new = _ffn(x1, wts["norm2"], wts["norm_f"], ffn0, wts["w_fconv"], wts["w_up"], wts["w_down"],
                       layer, bsz, seq, _tile(seq, 512), final_norm)
    state = (kk.reshape(bsz, seq, DA_HEADS, 2 * DA_DQK), vv.reshape(bsz, seq, DA_HEADS, DA_DV),
             c_new, n_new, m_new[:, :, 0], conv_new, ffn_new)
    return x2, state


def kernel(x_prompt, x_sample, cache_k, cache_v, state_C, state_n, state_m, state_conv, state_ffn,
           norm1, w_in, b_if, ml_norm, w_sc_conv, da_lambda, da_norm, w_branch, w_out,
           norm2, w_up, w_fconv, w_down, norm_f):
    depth, d, in_w = w_in.shape
    assert in_w == Z_W + GATE_W and d == 1024
    bp, sp, _ = x_prompt.shape
    bs, ss, _ = x_sample.shape
    past_len = cache_k.shape[2]
    g0 = 4 * HEAD_W

    w_gate = jnp.pad(w_in[:, :, g0:g0 + GATE_W], ((0, 0), (0, 0), (0, LANES - GATE_W))).astype(BF16)
    bias = jnp.concatenate([b_if[:, 0], b_if[:, 1]], axis=-1).astype(F32)
    wts = {
        "norm1": norm1.reshape(depth, 1, d),
        "w_main": jnp.concatenate([w_in[:, :, :g0], w_in[:, :, g0 + GATE_W:]], axis=-1).astype(BF16),
        "w_gate": w_gate,
        "w_gate_t": jnp.swapaxes(w_gate, 1, 2),
        "bias_row": jnp.pad(bias, ((0, 0), (0, LANES - GATE_W))).reshape(depth, 1, LANES),
        "bias_col": bias.reshape(depth, GATE_W, 1),
        "ml_norm": ml_norm,
        "w_sc_conv": w_sc_conv,
        "da_lambda": da_lambda,
        "da_norm": da_norm.reshape(depth, 1, DA_DV),
        "w_branch": w_branch.astype(BF16),
        "w_out": w_out.astype(BF16),
        "norm2": norm2.reshape(depth, 1, d),
        "w_up": w_up.astype(BF16),
        "w_fconv": w_fconv,
        "w_down": w_down.astype(BF16),
        "norm_f": norm_f.reshape(1, d),
    }
    past = {
        "k": cache_k.reshape(depth, bs, past_len, DA_HEADS * 2 * DA_DQK),
        "v": cache_v.reshape(depth, bs, past_len, DA_HEADS * DA_DV),
        "C": state_C, "n": state_n, "m": state_m, "conv": state_conv, "ffn": state_ffn,
    }

    xp = x_prompt.reshape(bp * sp, d)
    xs = x_sample.reshape(bs * ss, d)
    p_states, s_states = [], []
    for layer in range(depth):
        last = layer == depth - 1
        xp, st = _group_layer(xp, layer, bp, sp, wts, None, last)
        p_states.append(st)
        xs, st = _group_layer(xs, layer, bs, ss, wts, past, last)
        s_states.append(st)
    p_out = [jnp.stack(t) for t in zip(*p_states)]
    s_out = [jnp.stack(t) for t in zip(*s_states)]
    return (xp.reshape(bp, sp, d), xs.reshape(bs, ss, d), *p_out, *s_out)
```

```python
import functools
import math

import jax
import jax.numpy as jnp
from jax import lax
from jax.experimental import pallas as pl
from jax.experimental.pallas import tpu as pltpu

F32 = jnp.float32
BF16 = jnp.bfloat16

EPS = 1e-6
ML_HEADS = 4
ML_D = 128
SC_W = 512
DA_HEADS = 4
DA_DQK = 64
DA_DV = 128
ATT_CHUNK = 64
CONV_W = 3
HIST = CONV_W - 1
HIST_ROW = 8 - HIST
GATE_W = 2 * ML_HEADS
LANES = 128
HEAD_W = ML_HEADS * ML_D
Z_W = 10 * HEAD_W + 3 * 1024
LN_K_SCALE = -0.5 * math.log(ML_D)
Q_SCALE = DA_DQK ** -0.5 * math.log2(math.e)
VMEM_LIMIT = 56 * 2 ** 20

Z_MQ, Z_MK, Z_MV, Z_MO, Z_SB, Z_SC, Z_SX, Z_DQ, Z_DK, Z_DV = range(10)
Z_GT = 5


def _params(n_axes):
    return pltpu.CompilerParams(dimension_semantics=("arbitrary",) * n_axes,
                                vmem_limit_bytes=VMEM_LIMIT)


def _resident(block_shape, index_map):
    return pl.BlockSpec(block_shape, index_map, pipeline_mode=pl.Buffered(1))


def _rms(x, g):
    ms = jnp.mean(x * x, axis=-1, keepdims=True)
    return x * lax.rsqrt(ms + EPS) * g


def _dot(a, b):
    return jnp.dot(a, b, preferred_element_type=F32)


def _dot_nt(a, b):
    return lax.dot_general(a, b, (((1,), (1,)), ((), ())), preferred_element_type=F32)


def _dot_tn(a, b):
    return lax.dot_general(a, b, (((0,), (0,)), ((), ())), preferred_element_type=F32)


def _inproj_kernel(x_ref, g_ref, w_ref, wg_ref, wgt_ref, z_ref, k_ref, v_ref, gate_ref, gatet_ref):
    h = _rms(x_ref[...], g_ref[...]).astype(BF16)
    for c in range(Z_W // HEAD_W):
        cols = slice(c * HEAD_W, (c + 1) * HEAD_W)
        zc = _dot(h, w_ref[:, cols])
        if c == Z_DQ:
            zc = zc * Q_SCALE
        z_ref[:, cols] = zc.astype(BF16)
        if c == Z_DK:
            k_ref[...] = zc
        if c == Z_DV:
            v_ref[...] = zc
    gate_ref[...] = _dot(h, wg_ref[...])
    gatet_ref[...] = _dot_nt(wgt_ref[...], h)[:GATE_W, :]


def _in_proj(x2d, norm1, w_main, w_gate, w_gate_t, layer, tm):
    n, d = x2d.shape
    row = lambda i: (i, 0)
    lay3 = lambda i: (layer, 0, 0)
    return pl.pallas_call(
        _inproj_kernel,
        grid=(n // tm,),
        in_specs=[
            pl.BlockSpec((tm, d), row),
            pl.BlockSpec((None, 1, d), lay3),
            _resident((None, d, Z_W), lay3),
            _resident((None, d, LANES), lay3),
            _resident((None, LANES, d), lay3),
        ],
        out_specs=[
            pl.BlockSpec((tm, Z_W), row),
            pl.BlockSpec((tm, HEAD_W), row),
            pl.BlockSpec((tm, HEAD_W), row),
            pl.BlockSpec((tm, LANES), row),
            pl.BlockSpec((GATE_W, tm), lambda i: (0, i)),
        ],
        out_shape=[
            jax.ShapeDtypeStruct((n, Z_W), BF16),
            jax.ShapeDtypeStruct((n, HEAD_W), F32),
            jax.ShapeDtypeStruct((n, HEAD_W), F32),
            jax.ShapeDtypeStruct((n, LANES), F32),
            jax.ShapeDtypeStruct((GATE_W, n), F32),
        ],
        compiler_params=_params(1),
        name="in_proj",
    )(x2d, norm1, w_main, w_gate, w_gate_t)


def _log_sigmoid(x):
    return jnp.minimum(x, 0.0) - jnp.log1p(jnp.exp(-jnp.abs(x)))


def _split_bf16(x):
    hi = x.astype(BF16)
    lo = (x - hi.astype(F32)).astype(BF16)
    return hi, lo


def _mlstm_kernel(q_ref, k_ref, v_ref, o_ref, gate_ref, gatet_ref, brow_ref, bcol_ref, mlg_ref,
                  c0_ref, n0_ref, m0_ref, ya_ref, c_ref, n_ref, m_ref, *, length):
    L = length

    @pl.when(pl.program_id(1) == 0)
    def _():
        c_ref[...] = c0_ref[...]
        n_ref[...] = n0_ref[...]
        m_ref[...] = m0_ref[...]

    rows = lax.broadcasted_iota(jnp.int32, (L, L), 0)
    cols = lax.broadcasted_iota(jnp.int32, (L, L), 1)
    causal = rows >= cols
    tri = jnp.where(causal, 1.0, 0.0).astype(BF16)

    pre_c = gate_ref[...] + brow_ref[...]
    pre_r = gatet_ref[...] + bcol_ref[...]
    hi, lo = _split_bf16(_log_sigmoid(pre_c))
    cum_c = _dot(tri, hi) + _dot(tri, lo)
    hi, lo = _split_bf16(_log_sigmoid(pre_r))
    cum_r = _dot_nt(hi, tri) + _dot_nt(lo, tri)

    for h in range(ML_HEADS):
        hs = slice(h * ML_D, (h + 1) * ML_D)
        q = q_ref[:, hs]
        k = k_ref[:, hs]
        v = v_ref[:, hs]
        b_col = cum_c[:, ML_HEADS + h:ML_HEADS + h + 1]
        c_col = pre_c[:, h:h + 1] - b_col
        c_row = pre_r[h:h + 1, :] - cum_r[ML_HEADS + h:ML_HEADS + h + 1, :]
        m_prev = m_ref[0, h:h + 1, 0:1]
        c_prev = c_ref[0, h]
        n_prev = n_ref[0, h:h + 1, :]

        d = jnp.where(causal, c_row, -jnp.inf)
        mx = jnp.maximum(jnp.max(d, axis=1, keepdims=True), m_prev)
        w = jnp.exp(d + LN_K_SCALE - mx)
        inter = jnp.exp(m_prev - mx)
        s = _dot_nt(q, k) * w
        num = _dot(s.astype(BF16), v) + inter * _dot(q, c_prev.astype(BF16))
        qn = jnp.sum(q.astype(F32) * n_prev, axis=1, keepdims=True)
        den = jnp.sum(s, axis=1, keepdims=True) + inter * qn
        den = jnp.maximum(jnp.abs(den), jnp.exp(-(b_col + mx)))
        hn = _rms(num / den, mlg_ref[h:h + 1, :])
        ya = jax.nn.sigmoid(o_ref[:, hs].astype(F32)) * hn
        ya_ref[:, hs] = ya.astype(BF16)

        mx_last = mx[L - 1:L, :]
        wl = jnp.exp(c_col + LN_K_SCALE - mx_last)
        il = jnp.exp(m_prev - mx_last)
        kw = k.astype(F32) * wl
        c_ref[0, h] = il * c_prev + _dot_tn(kw.astype(BF16), v)
        n_ref[0, h:h + 1, :] = il * n_prev + jnp.sum(kw, axis=0, keepdims=True)
        m_ref[0, h:h + 1, :] = jnp.broadcast_to(b_col[L - 1:L, :] + mx_last, (1, LANES))


def _mlstm(z, gate, gate_t, bias_row, bias_col, ml_g, c0, n0, m0, layer, bsz, seq, length):
    n = bsz * seq
    nc = seq // length
    row = lambda col: (lambda b, j: (b * nc + j, col))
    lay3 = lambda b, j: (layer, 0, 0)
    state4 = lambda b, j: (b, 0, 0, 0)
    state3 = lambda b, j: (b, 0, 0)
    return pl.pallas_call(
        functools.partial(_mlstm_kernel, length=length),
        grid=(bsz, nc),
        in_specs=[
            pl.BlockSpec((length, HEAD_W), row(Z_MQ)),
            pl.BlockSpec((length, HEAD_W), row(Z_MK)),
            pl.BlockSpec((length, HEAD_W), row(Z_MV)),
            pl.BlockSpec((length, HEAD_W), row(Z_MO)),
            pl.BlockSpec((length, LANES), row(0)),
            pl.BlockSpec((None, GATE_W, length), lambda b, j: (b * nc + j, 0, 0)),
            pl.BlockSpec((None, 1, LANES), lay3),
            pl.BlockSpec((None, GATE_W, 1), lay3),
            pl.BlockSpec((None, ML_HEADS, ML_D), lay3),
            pl.BlockSpec((1, ML_HEADS, ML_D, ML_D), state4),
            pl.BlockSpec((1, ML_HEADS, ML_D), state3),
            pl.BlockSpec((1, ML_HEADS, LANES), state3),
        ],
        out_specs=[
            pl.BlockSpec((length, HEAD_W), row(0)),
            pl.BlockSpec((1, ML_HEADS, ML_D, ML_D), state4),
            pl.BlockSpec((1, ML_HEADS, ML_D), state3),
            pl.BlockSpec((1, ML_HEADS, LANES), state3),
        ],
        out_shape=[
            jax.ShapeDtypeStruct((n, HEAD_W), BF16),
            jax.ShapeDtypeStruct((bsz, ML_HEADS, ML_D, ML_D), F32),
            jax.ShapeDtypeStruct((bsz, ML_HEADS, ML_D), F32),
            jax.ShapeDtypeStruct((bsz, ML_HEADS, LANES), F32),
        ],
        compiler_params=_params(2),
        name="mlstm",
    )(z, z, z, z, gate, gate_t, bias_row, bias_col, ml_g, c0, n0, m0)


def _lambda(lam_ref, lam_init):
    lp = lam_ref[...]
    a = jnp.sum(lp[0:1] * lp[1:2], axis=1, keepdims=True)
    b = jnp.sum(lp[2:3] * lp[3:4], axis=1, keepdims=True)
    return jnp.exp(a) - jnp.exp(b) + lam_init


def _split_q(q):
    lane = lax.broadcasted_iota(jnp.int32, q.shape, 1)
    zero = jnp.zeros_like(q)
    return jnp.where(lane < DA_DQK, q, zero), jnp.where(lane >= DA_DQK, q, zero)


def _attn_finish(acc0, l0, acc1, l1, lam, dag, lam_init):
    o = acc0 / l0 - lam * (acc1 / l1)
    return _rms(o, dag) * (1.0 - lam_init)


def _attn_prompt_kernel(lam_ref, dag_ref, q_ref, k_ref, v_ref, o_ref, *, blk, nq, lam_init):
    lam = _lambda(lam_ref, lam_init)
    rows = lax.broadcasted_iota(jnp.int32, (2 * blk, blk), 0)
    cols = lax.broadcasted_iota(jnp.int32, (2 * blk, blk), 1)
    mask = (cols // ATT_CHUNK) <= ((rows % blk) // ATT_CHUNK)

    def attend(c):
        diag = slice(c * blk, (c + 1) * blk)
        q2 = jnp.concatenate(_split_q(q_ref[diag, :]), axis=0)
        sd = jnp.where(mask, _dot_nt(q2, k_ref[diag, :]), -jnp.inf)
        m = jnp.max(sd, axis=1, keepdims=True)
        if c > 0:
            so = _dot_nt(q2, k_ref[0:c * blk, :])
            m = jnp.maximum(m, jnp.max(so, axis=1, keepdims=True))
            po = jnp.exp2(so - m)
        pd = jnp.exp2(sd - m)
        l = jnp.sum(pd, axis=1, keepdims=True)
        if c > 0:
            l = l + jnp.sum(po, axis=1, keepdims=True)
        rho = lam * l[:blk] / l[blk:]
        o = _dot((pd[:blk] - rho * pd[blk:]).astype(BF16), v_ref[diag, :])
        if c > 0:
            o = o + _dot((po[:blk] - rho * po[blk:]).astype(BF16), v_ref[0:c * blk, :])
        o = o / l[:blk]
        o_ref[diag, :] = (_rms(o, dag_ref[...]) * (1.0 - lam_init)).astype(BF16)

    for t in range(nq // 2):
        @pl.when(pl.program_id(2) == t)
        def _():
            attend(t)
            attend(nq - 1 - t)


def _attn_prompt(z, da_lambda, da_norm, layer, bsz, seq, blk):
    nq = seq // blk
    assert nq % 2 == 0
    lam_init = 0.8 - 0.6 * math.exp(-0.3 * layer)
    lay3 = lambda b, h, t: (layer, 0, 0)
    whole = lambda slab: pl.BlockSpec((seq, DA_DV), lambda b, h, t: (b, slab * DA_HEADS + h))
    return pl.pallas_call(
        functools.partial(_attn_prompt_kernel, blk=blk, nq=nq, lam_init=lam_init),
        grid=(bsz, DA_HEADS, nq // 2),
        in_specs=[
            pl.BlockSpec((None, 4, DA_DQK), lay3),
            pl.BlockSpec((None, 1, DA_DV), lay3),
            whole(Z_DQ), whole(Z_DK), whole(Z_DV),
        ],
        out_specs=pl.BlockSpec((seq, DA_DV), lambda b, h, t: (b, h)),
        out_shape=jax.ShapeDtypeStruct((bsz * seq, HEAD_W), BF16),
        compiler_params=_params(3),
        name="attn_prompt",
    )(da_lambda, da_norm, z, z, z)


def _attn_sample_kernel(lam_ref, dag_ref, q_ref, kn_ref, vn_ref, kp_ref, vp_ref, o_ref, *, lam_init):
    q0, q1 = _split_q(q_ref[...])
    kp = kp_ref[...].astype(BF16)
    vp = vp_ref[...].astype(BF16)
    kn = kn_ref[...]
    vn = vn_ref[...]
    res = []
    for qc in (q0, q1):
        sp = _dot_nt(qc, kp)
        sn = _dot_nt(qc, kn)
        m = jnp.maximum(jnp.max(sp, axis=1, keepdims=True), jnp.max(sn, axis=1, keepdims=True))
        pp = jnp.exp2(sp - m)
        pn = jnp.exp2(sn - m)
        l = jnp.sum(pp, axis=1, keepdims=True) + jnp.sum(pn, axis=1, keepdims=True)
        acc = _dot(pp.astype(BF16), vp) + _dot(pn.astype(BF16), vn)
        res.append((acc, l))
    (acc0, l0), (acc1, l1) = res
    out = _attn_finish(acc0, l0, acc1, l1, _lambda(lam_ref, lam_init), dag_ref[...], lam_init)
    o_ref[...] = out.astype(BF16)


def _attn_sample(z, cache_k, cache_v, da_lambda, da_norm, layer, bsz, seq):
    past = cache_k.shape[2]
    lam_init = 0.8 - 0.6 * math.exp(-0.3 * layer)
    lay3 = lambda b, h: (layer, 0, 0)
    cache = lambda b, h: (layer, b, 0, h)
    return pl.pallas_call(
        functools.partial(_attn_sample_kernel, lam_init=lam_init),
        grid=(bsz, DA_HEADS),
        in_specs=[
            pl.BlockSpec((None, 4, DA_DQK), lay3),
            pl.BlockSpec((None, 1, DA_DV), lay3),
            pl.BlockSpec((seq, DA_DV), lambda b, h: (b, Z_DQ * DA_HEADS + h)),
            pl.BlockSpec((seq, DA_DV), lambda b, h: (b, Z_DK * DA_HEADS + h)),
            pl.BlockSpec((seq, DA_DV), lambda b, h: (b, Z_DV * DA_HEADS + h)),
            pl.BlockSpec((None, None, past, DA_DV), cache),
            pl.BlockSpec((None, None, past, DA_DV), cache),
        ],
        out_specs=pl.BlockSpec((seq, DA_DV), lambda b, h: (b, h)),
        out_shape=jax.ShapeDtypeStruct((bsz * seq, HEAD_W), BF16),
        compiler_params=_params(2),
        name="attn_sample",
    )(da_lambda, da_norm, z, z, z, cache_k, cache_v)


def _load_history(buf_ref, hist_ref, tm):
    @pl.when(pl.program_id(1) == 0)
    def _():
        buf_ref[HIST_ROW:8, :] = hist_ref[0]

    @pl.when(pl.program_id(1) > 0)
    def _():
        buf_ref[HIST_ROW:8, :] = buf_ref[tm + HIST_ROW:tm + 8, :]


def _merge_kernel(x_ref, sb_ref, sc_ref, sx_ref, g0_ref, g1_ref, g2_ref, ya_ref, yc_ref, hist_ref,
                  scw_ref, wbr_ref, wo_ref, x1_ref, cs_ref, ubuf, *, tm):
    _load_history(ubuf, hist_ref, tm)
    u = sc_ref[...].astype(F32) * sx_ref[...].astype(F32)
    ubuf[8:8 + tm, :] = u
    cu = (scw_ref[2:3, :] * u + scw_ref[1:2, :] * ubuf[7:7 + tm, :]
          + scw_ref[0:1, :] * ubuf[6:6 + tm, :])
    cs_ref[0] = ubuf[tm + HIST_ROW:tm + 8, :]
    yb = (sb_ref[...].astype(F32) * cu).astype(BF16)
    mix = (jax.nn.sigmoid(g0_ref[...].astype(F32)) * _dot(ya_ref[...], wbr_ref[0])
           + jax.nn.sigmoid(g1_ref[...].astype(F32)) * _dot(yb, wbr_ref[1])
           + jax.nn.sigmoid(g2_ref[...].astype(F32)) * _dot(yc_ref[...], wbr_ref[2]))
    x1_ref[...] = x_ref[...] + _dot(mix.astype(BF16), wo_ref[...])


def _merge(x2d, z, ya, yc, hist, sc_w, w_br, w_o, layer, bsz, seq, tm):
    n, d = x2d.shape
    nt = seq // tm
    row = lambda col: (lambda b, j: (b * nt + j, col))
    lay3 = lambda b, j: (layer, 0, 0)
    state = lambda b, j: (b, 0, 0)
    return pl.pallas_call(
        functools.partial(_merge_kernel, tm=tm),
        grid=(bsz, nt),
        in_specs=[
            pl.BlockSpec((tm, d), row(0)),
            pl.BlockSpec((tm, HEAD_W), row(Z_SB)),
            pl.BlockSpec((tm, HEAD_W), row(Z_SC)),
            pl.BlockSpec((tm, HEAD_W), row(Z_SX)),
            pl.BlockSpec((tm, d), row(Z_GT)),
            pl.BlockSpec((tm, d), row(Z_GT + 1)),
            pl.BlockSpec((tm, d), row(Z_GT + 2)),
            pl.BlockSpec((tm, HEAD_W), row(0)),
            pl.BlockSpec((tm, HEAD_W), row(0)),
            pl.BlockSpec((1, HIST, SC_W), state),
            pl.BlockSpec((None, CONV_W, SC_W), lay3),
            _resident((None, 3, HEAD_W, d), lambda b, j: (layer, 0, 0, 0)),
            _resident((None, d, d), lay3),
        ],
        out_specs=[
            pl.BlockSpec((tm, d), row(0)),
            pl.BlockSpec((1, HIST, SC_W), state),
        ],
        out_shape=[
            jax.ShapeDtypeStruct((n, d), F32),
            jax.ShapeDtypeStruct((bsz, HIST, SC_W), F32),
        ],
        scratch_shapes=[pltpu.VMEM((tm + 8, SC_W), F32)],
        compiler_params=_params(2),
        name="merge",
    )(x2d, z, z, z, z, z, z, ya, yc, hist, sc_w, w_br, w_o)


def _ffn_kernel(x_ref, g2_ref, gf_ref, hist_ref, fw_ref, wup_ref, wdn_ref, y_ref, fs_ref, upbuf,
                *, tm, d_ff, chunk, final_norm):
    _load_history(upbuf, hist_ref, tm)
    x = x_ref[...]
    h2 = _rms(x, g2_ref[...]).astype(BF16)

    def conv_up(cols):
        up = _dot(h2, wup_ref[:, cols])
        upbuf[8:8 + tm, cols] = up
        return (fw_ref[2:3, cols] * up + fw_ref[1:2, cols] * upbuf[7:7 + tm, cols]
                + fw_ref[0:1, cols] * upbuf[6:6 + tm, cols])

    acc = jnp.zeros_like(x)
    for c in range(0, d_ff, chunk):
        a = conv_up(slice(c, c + chunk))
        b = conv_up(slice(d_ff + c, d_ff + c + chunk))
        act = (a * jax.nn.sigmoid(a) * b).astype(BF16)
        acc = acc + _dot(act, wdn_ref[c:c + chunk, :])
    fs_ref[0] = upbuf[tm + HIST_ROW:tm + 8, :]
    y = x + acc
    if final_norm:
        y = _rms(y, gf_ref[...])
    y_ref[...] = y


def _ffn(x2d, norm2, norm_f, hist, w_fconv, w_up, w_down, layer, bsz, seq, tm, final_norm):
    n, d = x2d.shape
    d_ff = w_down.shape[1]
    nt = seq // tm
    chunk = 256 if d_ff % 256 == 0 else LANES
    row = lambda b, j: (b * nt + j, 0)
    lay3 = lambda b, j: (layer, 0, 0)
    state = lambda b, j: (b, 0, 0)
    return pl.pallas_call(
        functools.partial(_ffn_kernel, tm=tm, d_ff=d_ff, chunk=chunk, final_norm=final_norm),
        grid=(bsz, nt),
        in_specs=[
            pl.BlockSpec((tm, d), row),
            pl.BlockSpec((None, 1, d), lay3),
            pl.BlockSpec((1, d), lambda b, j: (0, 0)),
            pl.BlockSpec((1, HIST, 2 * d_ff), state),
            pl.BlockSpec((None, CONV_W, 2 * d_ff), lay3),
            _resident((None, d, 2 * d_ff), lay3),
            _resident((None, d_ff, d), lay3),
        ],
        out_specs=[
            pl.BlockSpec((tm, d), row),
            pl.BlockSpec((1, HIST, 2 * d_ff), state),
        ],
        out_shape=[
            jax.ShapeDtypeStruct((n, d), F32),
            jax.ShapeDtypeStruct((bsz, HIST, 2 * d_ff), F32),
        ],
        scratch_shapes=[pltpu.VMEM((tm + 8, 2 * d_ff), F32)],
        compiler_params=_params(2),
        name="ffn",
    )(x2d, norm2, norm_f, hist, w_fconv, w_up, w_down)


def _tile(seq, target):
    return target if seq % target == 0 else seq


def _group_layer(x2d, layer, bsz, seq, wts, past, final_norm):
    d = x2d.shape[1]
    z, kk, vv, gate, gate_t = _in_proj(x2d, wts["norm1"], wts["w_main"], wts["w_gate"],
                                       wts["w_gate_t"], layer, _tile(bsz * seq, 512))
    if past is None:
        c0 = jnp.zeros((bsz, ML_HEADS, ML_D, ML_D), F32)
        n0 = jnp.zeros((bsz, ML_HEADS, ML_D), F32)
        m0 = jnp.zeros((bsz, ML_HEADS, LANES), F32)
        conv0 = jnp.zeros((bsz, HIST, SC_W), F32)
        ffn0 = jnp.zeros((bsz, HIST, wts["w_fconv"].shape[2]), F32)
    else:
        c0, n0 = past["C"][layer], past["n"][layer]
        m0 = jnp.broadcast_to(past["m"][layer][..., None], (bsz, ML_HEADS, LANES))
        conv0, ffn0 = past["conv"][layer], past["ffn"][layer]
    length = _tile(seq, 256)
    gate_t = gate_t.reshape(GATE_W, bsz * seq // length, length).swapaxes(0, 1)
    ya, c_new, n_new, m_new = _mlstm(z, gate, gate_t, wts["bias_row"], wts["bias_col"], wts["ml_norm"],
                                     c0, n0, m0, layer, bsz, seq, length)
    if past is None:
        yc = _attn_prompt(z, wts["da_lambda"], wts["da_norm"], layer, bsz, seq, 256)
    else:
        yc = _attn_sample(z, past["k"], past["v"], wts["da_lambda"], wts["da_norm"], layer, bsz, seq)
    x1, conv_new = _merge(x2d, z, ya, yc, conv0, wts["w_sc_conv"], wts["w_branch"], wts["w_out"],
                          layer, bsz, seq, _tile(seq, 512))
    x2, ffn_new = _ffn(x1, wts["norm2"], wts["norm_f"], ffn0, wts["w_fconv"], wts["w_up"], wts["w_down"],
                       layer, bsz, seq, _tile(seq, 512), final_norm)
    state = (kk.reshape(bsz, seq, DA_HEADS, 2 * DA_DQK), vv.reshape(bsz, seq, DA_HEADS, DA_DV),
             c_new, n_new, m_new[:, :, 0], conv_new, ffn_new)
    return x2, state


def kernel(x_prompt, x_sample, cache_k, cache_v, state_C, state_n, state_m, state_conv, state_ffn,
           norm1, w_in, b_if, ml_norm, w_sc_conv, da_lambda, da_norm, w_branch, w_out,
           norm2, w_up, w_fconv, w_down, norm_f):
    depth, d, in_w = w_in.shape
    assert in_w == Z_W + GATE_W and d == 1024
    bp, sp, _ = x_prompt.shape
    bs, ss, _ = x_sample.shape
    past_len = cache_k.shape[2]
    g0 = 4 * HEAD_W

    w_gate = jnp.pad(w_in[:, :, g0:g0 + GATE_W], ((0, 0), (0, 0), (0, LANES - GATE_W))).astype(BF16)
    bias = jnp.concatenate([b_if[:, 0], b_if[:, 1]], axis=-1).astype(F32)
    wts = {
        "norm1": norm1.reshape(depth, 1, d),
        "w_main": jnp.concatenate([w_in[:, :, :g0], w_in[:, :, g0 + GATE_W:]], axis=-1).astype(BF16),
        "w_gate": w_gate,
        "w_gate_t": jnp.swapaxes(w_gate, 1, 2),
        "bias_row": jnp.pad(bias, ((0, 0), (0, LANES - GATE_W))).reshape(depth, 1, LANES),
        "bias_col": bias.reshape(depth, GATE_W, 1),
        "ml_norm": ml_norm,
        "w_sc_conv": w_sc_conv,
        "da_lambda": da_lambda,
        "da_norm": da_norm.reshape(depth, 1, DA_DV),
        "w_branch": w_branch.astype(BF16),
        "w_out": w_out.astype(BF16),
        "norm2": norm2.reshape(depth, 1, d),
        "w_up": w_up.astype(BF16),
        "w_fconv": w_fconv,
        "w_down": w_down.astype(BF16),
        "norm_f": norm_f.reshape(1, d),
    }
    past = {
        "k": cache_k.reshape(depth, bs, past_len, DA_HEADS * 2 * DA_DQK),
        "v": cache_v.reshape(depth, bs, past_len, DA_HEADS * DA_DV),
        "C": state_C, "n": state_n, "m": state_m, "conv": state_conv, "ffn": state_ffn,
    }

    xp = x_prompt.reshape(bp * sp, d)
    xs = x_sample.reshape(bs * ss, d)
    p_states, s_states = [], []
    for layer in range(depth):
        last = layer == depth - 1
        xp, st = _group_layer(xp, layer, bp, sp, wts, None, last)
        p_states.append(st)
        xs, st = _group_layer(xs, layer, bs, ss, wts, past, last)
        s_states.append(st)
    p_out = [jnp.stack(t) for t in zip(*p_states)]
    s_out = [jnp.stack(t) for t in zip(*s_states)]
    return (xp.reshape(bp, sp, d), xs.reshape(bs, ss, d), *p_out, *s_out)
```

```python
import functools
import math

import jax
import jax.numpy as jnp
from jax import lax
from jax.experimental import pallas as pl
from jax.experimental.pallas import tpu as pltpu

F32 = jnp.float32
BF16 = jnp.bfloat16

EPS = 1e-6
ML_HEADS = 4
ML_D = 128
SC_W = 512
DA_HEADS = 4
DA_DQK = 64
DA_DV = 128
ATT_CHUNK = 64
CONV_W = 3
HIST = CONV_W - 1
FFN_DOWN_GROUP = 4
GATE_W = 2 * ML_HEADS
LANES = 128
HEAD_W = ML_HEADS * ML_D
Z_W = 10 * HEAD_W + 3 * 1024
LN_K_SCALE = -0.5 * math.log(ML_D)
Q_SCALE = DA_DQK ** -0.5 * math.log2(math.e)
VMEM_LIMIT = 56 * 2 ** 20

Z_MQ, Z_MK, Z_MV, Z_MO, Z_SB, Z_SC, Z_SX, Z_DQ, Z_DK, Z_DV = range(10)
Z_GT = 5


def _params(n_axes):
    return pltpu.CompilerParams(dimension_semantics=("arbitrary",) * n_axes,
                                vmem_limit_bytes=VMEM_LIMIT)


def _resident(block_shape, index_map):
    return pl.BlockSpec(block_shape, index_map, pipeline_mode=pl.Buffered(1))


def _rms(x, g):
    ms = jnp.mean(x * x, axis=-1, keepdims=True)
    return x * lax.rsqrt(ms + EPS) * g


def _sigmoid(x):
    return 0.5 * jnp.tanh(0.5 * x) + 0.5


def _dot(a, b):
    return jnp.dot(a, b, preferred_element_type=F32)


def _dot_nt(a, b):
    return lax.dot_general(a, b, (((1,), (1,)), ((), ())), preferred_element_type=F32)


def _dot_tn(a, b):
    return lax.dot_general(a, b, (((0,), (0,)), ((), ())), preferred_element_type=F32)


def _inproj_kernel(x_ref, g_ref, wa_ref, wb_ref, wg_ref, wgt_ref, *rest, tm, first):
    z_ref, k_ref, v_ref, gate_ref, gatet_ref = rest if first else rest[2:]
    h = _rms(x_ref[...], g_ref[...]).astype(BF16)
    for c in range(Z_W // HEAD_W):
        cols = slice(c * HEAD_W, (c + 1) * HEAD_W)
        w = wa_ref[:, cols] if c < Z_SB else wb_ref[:, (c - Z_SB) * HEAD_W:(c - Z_SB + 1) * HEAD_W]
        zc = _dot(h, w)
        if c == Z_DQ:
            zc = zc * Q_SCALE
        z_ref[:, cols] = zc.astype(BF16)
        if c == Z_DK:
            _store_heads(k_ref, zc, tm, first)
        if c == Z_DV:
            _store_heads(v_ref, zc, tm, first)
    gate_ref[...] = _dot(h, wg_ref[...])
    gatet_ref[...] = _dot_nt(wgt_ref[...], h)[:GATE_W, :]


def _store_heads(ref, slab, tm, first):
    dst = ref.at[0] if first else ref
    for h in range(DA_HEADS):
        dst[pl.ds(h, tm, stride=DA_HEADS), :] = slab[:, h * DA_DV:(h + 1) * DA_DV]
    if first:
        ref[1:] = jnp.zeros((ref.shape[0] - 1,) + ref.shape[1:], F32)


def _in_proj(x2d, norm1, w_head, w_tail, w_gate, w_gate_t, kv_bufs, layer, depth, tm):
    n, d = x2d.shape
    first = kv_bufs is None
    row = lambda i: (i, 0)
    lay3 = lambda i: (layer, 0, 0)
    if first:
        kv_spec = pl.BlockSpec((depth, tm * DA_HEADS, DA_DV), lambda i: (0, i, 0))
        extra_specs, extra_args, aliases = [], (), {}
    else:
        kv_spec = pl.BlockSpec((None, tm * DA_HEADS, DA_DV), lambda i: (layer, i, 0))
        extra_specs = [pl.BlockSpec(memory_space=pl.ANY)] * 2
        extra_args, aliases = tuple(kv_bufs), {6: 1, 7: 2}
    kv_shape = jax.ShapeDtypeStruct((depth, n * DA_HEADS, DA_DV), F32)
    return pl.pallas_call(
        functools.partial(_inproj_kernel, tm=tm, first=first),
        grid=(n // tm,),
        in_specs=[
            pl.BlockSpec((tm, d), row),
            pl.BlockSpec((None, 1, d), lay3),
            _resident((None, d, Z_SB * HEAD_W), lay3),
            _resident((None, d, Z_W - Z_SB * HEAD_W), lay3),
            _resident((None, d, LANES), lay3),
            _resident((None, LANES, d), lay3),
        ] + extra_specs,
        out_specs=[
            pl.BlockSpec((tm, Z_W), row),
            kv_spec,
            kv_spec,
            pl.BlockSpec((tm, LANES), row),
            pl.BlockSpec((GATE_W, tm), lambda i: (0, i)),
        ],
        out_shape=[
            jax.ShapeDtypeStruct((n, Z_W), BF16),
            kv_shape,
            kv_shape,
            jax.ShapeDtypeStruct((n, LANES), F32),
            jax.ShapeDtypeStruct((GATE_W, n), F32),
        ],
        input_output_aliases=aliases,
        compiler_params=_params(1),
        name="in_proj",
    )(x2d, norm1, w_head, w_tail, w_gate, w_gate_t, *extra_args)


def _log_sigmoid(x):
    return jnp.minimum(x, 0.0) - jnp.log1p(jnp.exp(-jnp.abs(x)))


def _split_bf16(x):
    hi = x.astype(BF16)
    lo = (x - hi.astype(F32)).astype(BF16)
    return hi, lo


def _mlstm_kernel(q_ref, k_ref, v_ref, o_ref, gate_ref, gatet_ref, brow_ref, bcol_ref, mlg_ref,
                  c0_ref, n0_ref, m0_ref, ya_ref, c_ref, n_ref, m_ref, *, length):
    L = length

    @pl.when(pl.program_id(1) == 0)
    def _():
        c_ref[...] = c0_ref[...]
        n_ref[...] = n0_ref[...]
        m_ref[...] = m0_ref[...]

    rows = lax.broadcasted_iota(jnp.int32, (L, L), 0)
    cols = lax.broadcasted_iota(jnp.int32, (L, L), 1)
    causal = rows >= cols
    tri = jnp.where(causal, 1.0, 0.0).astype(BF16)

    pre_c = gate_ref[...] + brow_ref[...]
    pre_r = gatet_ref[...] + bcol_ref[...]
    hi, lo = _split_bf16(_log_sigmoid(pre_c))
    cum_c = _dot(tri, hi) + _dot(tri, lo)
    hi, lo = _split_bf16(_log_sigmoid(pre_r))
    cum_r = _dot_nt(hi, tri) + _dot_nt(lo, tri)

    heads = range(ML_HEADS)
    hs = [slice(h * ML_D, (h + 1) * ML_D) for h in heads]
    q = [q_ref[:, hs[h]] for h in heads]
    k = [k_ref[:, hs[h]] for h in heads]
    v = [v_ref[:, hs[h]] for h in heads]
    b_col = [cum_c[:, ML_HEADS + h:ML_HEADS + h + 1] for h in heads]
    c_col = [pre_c[:, h:h + 1] - b_col[h] for h in heads]
    c_row = [pre_r[h:h + 1, :] - cum_r[ML_HEADS + h:ML_HEADS + h + 1, :] for h in heads]
    m_prev = [m_ref[0, h:h + 1, 0:1] for h in heads]
    c_prev = [c_ref[0, h] for h in heads]
    n_prev = [n_ref[0, h:h + 1, :] for h in heads]

    d = [jnp.where(causal, c_row[h], -jnp.inf) for h in heads]
    mx = [jnp.maximum(jnp.max(d[h], axis=1, keepdims=True), m_prev[h]) for h in heads]
    w = [jnp.exp(d[h] + LN_K_SCALE - mx[h]) for h in heads]
    inter = [jnp.exp(m_prev[h] - mx[h]) for h in heads]
    s = [_dot_nt(q[h], k[h]) * w[h] for h in heads]
    num = [_dot(s[h].astype(BF16), v[h]) + inter[h] * _dot(q[h], c_prev[h].astype(BF16)) for h in heads]
    qn = [jnp.sum(q[h].astype(F32) * n_prev[h], axis=1, keepdims=True) for h in heads]
    den = [jnp.sum(s[h], axis=1, keepdims=True) + inter[h] * qn[h] for h in heads]
    den = [jnp.maximum(jnp.abs(den[h]), jnp.exp(-(b_col[h] + mx[h]))) for h in heads]
    hn = [_rms(num[h] / den[h], mlg_ref[h:h + 1, :]) for h in heads]
    for h in heads:
        ya_ref[:, hs[h]] = (_sigmoid(o_ref[:, hs[h]].astype(F32)) * hn[h]).astype(BF16)

    mx_last = [mx[h][L - 1:L, :] for h in heads]
    wl = [jnp.exp(c_col[h] + LN_K_SCALE - mx_last[h]) for h in heads]
    il = [jnp.exp(m_prev[h] - mx_last[h]) for h in heads]
    kw = [k[h].astype(F32) * wl[h] for h in heads]
    for h in heads:
        c_ref[0, h] = il[h] * c_prev[h] + _dot_tn(kw[h].astype(BF16), v[h])
        n_ref[0, h:h + 1, :] = il[h] * n_prev[h] + jnp.sum(kw[h], axis=0, keepdims=True)
        m_ref[0, h:h + 1, :] = jnp.broadcast_to(b_col[h][L - 1:L, :] + mx_last[h], (1, LANES))


def _mlstm(z, gate, gate_t, bias_row, bias_col, ml_g, c0, n0, m0, layer, bsz, seq, length):
    n = bsz * seq
    nc = seq // length
    row = lambda col: (lambda b, j: (b * nc + j, col))
    lay3 = lambda b, j: (layer, 0, 0)
    state4 = lambda b, j: (b, 0, 0, 0)
    state3 = lambda b, j: (b, 0, 0)
    return pl.pallas_call(
        functools.partial(_mlstm_kernel, length=length),
        grid=(bsz, nc),
        in_specs=[
            pl.BlockSpec((length, HEAD_W), row(Z_MQ)),
            pl.BlockSpec((length, HEAD_W), row(Z_MK)),
            pl.BlockSpec((length, HEAD_W), row(Z_MV)),
            pl.BlockSpec((length, HEAD_W), row(Z_MO)),
            pl.BlockSpec((length, LANES), row(0)),
            pl.BlockSpec((None, GATE_W, length), lambda b, j: (b * nc + j, 0, 0)),
            pl.BlockSpec((None, 1, LANES), lay3),
            pl.BlockSpec((None, GATE_W, 1), lay3),
            pl.BlockSpec((None, ML_HEADS, ML_D), lay3),
            pl.BlockSpec((1, ML_HEADS, ML_D, ML_D), state4),
            pl.BlockSpec((1, ML_HEADS, ML_D), state3),
            pl.BlockSpec((1, ML_HEADS, LANES), state3),
        ],
        out_specs=[
            pl.BlockSpec((length, HEAD_W), row(0)),
            pl.BlockSpec((1, ML_HEADS, ML_D, ML_D), state4),
            pl.BlockSpec((1, ML_HEADS, ML_D), state3),
            pl.BlockSpec((1, ML_HEADS, LANES), state3),
        ],
        out_shape=[
            jax.ShapeDtypeStruct((n, HEAD_W), BF16),
            jax.ShapeDtypeStruct((bsz, ML_HEADS, ML_D, ML_D), F32),
            jax.ShapeDtypeStruct((bsz, ML_HEADS, ML_D), F32),
            jax.ShapeDtypeStruct((bsz, ML_HEADS, LANES), F32),
        ],
        compiler_params=_params(2),
        name="mlstm",
    )(z, z, z, z, gate, gate_t, bias_row, bias_col, ml_g, c0, n0, m0)


def _lambda(lam_ref, lam_init):
    lp = lam_ref[...]
    a = jnp.sum(lp[0:1] * lp[1:2], axis=1, keepdims=True)
    b = jnp.sum(lp[2:3] * lp[3:4], axis=1, keepdims=True)
    return jnp.exp(a) - jnp.exp(b) + lam_init


def _split_q(q):
    lane = lax.broadcasted_iota(jnp.int32, q.shape, 1)
    zero = jnp.zeros_like(q)
    return jnp.where(lane < DA_DQK, q, zero), jnp.where(lane >= DA_DQK, q, zero)


def _attn_prompt_kernel(lam_ref, dag_ref, q_ref, k_ref, v_ref, o_ref, *, blk, nq, lam_init):
    lam = _lambda(lam_ref, lam_init)
    rows = lax.broadcasted_iota(jnp.int32, (2 * blk, blk), 0)
    cols = lax.broadcasted_iota(jnp.int32, (2 * blk, blk), 1)
    mask = (cols // ATT_CHUNK) <= ((rows % blk) // ATT_CHUNK)

    def attend(c):
        diag = slice(c * blk, (c + 1) * blk)
        q2 = jnp.concatenate(_split_q(q_ref[diag, :]), axis=0)
        sd = jnp.where(mask, _dot_nt(q2, k_ref[diag, :]), -jnp.inf)
        m = jnp.max(sd, axis=1, keepdims=True)
        if c > 0:
            so = _dot_nt(q2, k_ref[0:c * blk, :])
            m = jnp.maximum(m, jnp.max(so, axis=1, keepdims=True))
            po = jnp.exp2(so - m)
        pd = jnp.exp2(sd - m)
        l = jnp.sum(pd, axis=1, keepdims=True)
        if c > 0:
            l = l + jnp.sum(po, axis=1, keepdims=True)
        rho = lam * l[:blk] / l[blk:]
        o = _dot((pd[:blk] - rho * pd[blk:]).astype(BF16), v_ref[diag, :])
        if c > 0:
            o = o + _dot((po[:blk] - rho * po[blk:]).astype(BF16), v_ref[0:c * blk, :])
        o = o / l[:blk]
        o_ref[diag, :] = (_rms(o, dag_ref[...]) * (1.0 - lam_init)).astype(BF16)

    for t in range(nq // 2):
        @pl.when(pl.program_id(2) == t)
        def _():
            attend(t)
            attend(nq - 1 - t)


def _attn_prompt(z, da_lambda, da_norm, layer, bsz, seq, blk):
    nq = seq // blk
    assert nq % 2 == 0
    lam_init = 0.8 - 0.6 * math.exp(-0.3 * layer)
    lay3 = lambda b, h, t: (layer, 0, 0)
    whole = lambda slab: pl.BlockSpec((seq, DA_DV), lambda b, h, t: (b, slab * DA_HEADS + h))
    return pl.pallas_call(
        functools.partial(_attn_prompt_kernel, blk=blk, nq=nq, lam_init=lam_init),
        grid=(bsz, DA_HEADS, nq // 2),
        in_specs=[
            pl.BlockSpec((None, 4, DA_DQK), lay3),
            pl.BlockSpec((None, 1, DA_DV), lay3),
            whole(Z_DQ), whole(Z_DK), whole(Z_DV),
        ],
        out_specs=pl.BlockSpec((seq, DA_DV), lambda b, h, t: (b, h)),
        out_shape=jax.ShapeDtypeStruct((bsz * seq, HEAD_W), BF16),
        compiler_params=_params(3),
        name="attn_prompt",
    )(da_lambda, da_norm, z, z, z)


def _attn_sample_kernel(lam_ref, dag_ref, q_ref, kn_ref, vn_ref, kp_ref, vp_ref, o_ref, *, past, lam_init):
    lam = _lambda(lam_ref, lam_init)
    t = q_ref.shape[0]
    for h in range(DA_HEADS):
        hs = slice(h * DA_DV, (h + 1) * DA_DV)
        q2 = jnp.concatenate(_split_q(q_ref[:, hs]), axis=0)
        kp = kp_ref[pl.ds(h, past, stride=DA_HEADS), :].astype(BF16)
        vp = vp_ref[pl.ds(h, past, stride=DA_HEADS), :].astype(BF16)
        sp = _dot_nt(q2, kp)
        sn = _dot_nt(q2, kn_ref[:, hs])
        m = jnp.maximum(jnp.max(sp, axis=1, keepdims=True), jnp.max(sn, axis=1, keepdims=True))
        pp = jnp.exp2(sp - m)
        pn = jnp.exp2(sn - m)
        l = jnp.sum(pp, axis=1, keepdims=True) + jnp.sum(pn, axis=1, keepdims=True)
        rho = lam * l[:t] / l[t:]
        o = (_dot((pp[:t] - rho * pp[t:]).astype(BF16), vp)
             + _dot((pn[:t] - rho * pn[t:]).astype(BF16), vn_ref[:, hs]))
        o = o / l[:t]
        o_ref[:, hs] = (_rms(o, dag_ref[...]) * (1.0 - lam_init)).astype(BF16)


def _attn_sample(z, cache_k, cache_v, da_lambda, da_norm, layer, bsz, seq):
    past = cache_k.shape[2] // DA_HEADS
    lam_init = 0.8 - 0.6 * math.exp(-0.3 * layer)
    lay3 = lambda b: (layer, 0, 0)
    cache = pl.BlockSpec((None, None, past * DA_HEADS, DA_DV), lambda b: (layer, b, 0, 0))
    slab = lambda col: pl.BlockSpec((seq, HEAD_W), lambda b: (b, col))
    return pl.pallas_call(
        functools.partial(_attn_sample_kernel, past=past, lam_init=lam_init),
        grid=(bsz,),
        in_specs=[
            pl.BlockSpec((None, 4, DA_DQK), lay3),
            pl.BlockSpec((None, 1, DA_DV), lay3),
            slab(Z_DQ), slab(Z_DK), slab(Z_DV),
            cache, cache,
        ],
        out_specs=pl.BlockSpec((seq, HEAD_W), lambda b: (b, 0)),
        out_shape=jax.ShapeDtypeStruct((bsz * seq, HEAD_W), BF16),
        compiler_params=_params(1),
        name="attn_sample",
    )(da_lambda, da_norm, z, z, z, cache_k, cache_v)


def _causal_conv3(u, hist, w):
    row = lax.broadcasted_iota(jnp.int32, (8, u.shape[1]), 0)
    r1 = pltpu.roll(u, 1, 0)
    r2 = pltpu.roll(u, 2, 0)
    top1 = jnp.where(row == 0, hist[1:2], r1[:8])
    top2 = jnp.where(row == 0, hist[0:1], jnp.where(row == 1, hist[1:2], r2[:8]))
    x1 = jnp.concatenate([top1, r1[8:]], axis=0)
    x2 = jnp.concatenate([top2, r2[8:]], axis=0)
    return w[2:3] * u + w[1:2] * x1 + w[0:1] * x2


def _init_history(hbuf, hist_ref):
    @pl.when(pl.program_id(1) == 0)
    def _():
        hbuf[...] = hist_ref[0]


def _merge_kernel(x_ref, sb_ref, sc_ref, sx_ref, g0_ref, g1_ref, g2_ref, ya_ref, yc_ref, hist_ref,
                  scw_ref, wbr_ref, wo_ref, x1_ref, cs_ref, hbuf, *, tm):
    _init_history(hbuf, hist_ref)
    u = sc_ref[...].astype(F32) * sx_ref[...].astype(F32)
    cu = _causal_conv3(u, hbuf[...], scw_ref[...])
    hbuf[...] = u[tm - HIST:tm, :]
    cs_ref[0] = u[tm - HIST:tm, :]
    yb = (sb_ref[...].astype(F32) * cu).astype(BF16)
    mix = (_sigmoid(g0_ref[...].astype(F32)) * _dot(ya_ref[...], wbr_ref[0])
           + _sigmoid(g1_ref[...].astype(F32)) * _dot(yb, wbr_ref[1])
           + _sigmoid(g2_ref[...].astype(F32)) * _dot(yc_ref[...], wbr_ref[2]))
    x1_ref[...] = x_ref[...] + _dot(mix.astype(BF16), wo_ref[...])


def _merge(x2d, z, ya, yc, hist, sc_w, w_br, w_o, layer, bsz, seq, tm):
    n, d = x2d.shape
    nt = seq // tm
    row = lambda col: (lambda b, j: (b * nt + j, col))
    lay3 = lambda b, j: (layer, 0, 0)
    state = lambda b, j: (b, 0, 0)
    return pl.pallas_call(
        functools.partial(_merge_kernel, tm=tm),
        grid=(bsz, nt),
        in_specs=[
            pl.BlockSpec((tm, d), row(0)),
            pl.BlockSpec((tm, HEAD_W), row(Z_SB)),
            pl.BlockSpec((tm, HEAD_W), row(Z_SC)),
            pl.BlockSpec((tm, HEAD_W), row(Z_SX)),
            pl.BlockSpec((tm, d), row(Z_GT)),
            pl.BlockSpec((tm, d), row(Z_GT + 1)),
            pl.BlockSpec((tm, d), row(Z_GT + 2)),
            pl.BlockSpec((tm, HEAD_W), row(0)),
            pl.BlockSpec((tm, HEAD_W), row(0)),
            pl.BlockSpec((1, HIST, SC_W), state),
            pl.BlockSpec((None, CONV_W, SC_W), lay3),
            _resident((None, 3, HEAD_W, d), lambda b, j: (layer, 0, 0, 0)),
            _resident((None, d, d), lay3),
        ],
        out_specs=[
            pl.BlockSpec((tm, d), row(0)),
            pl.BlockSpec((1, HIST, SC_W), state),
        ],
        out_shape=[
            jax.ShapeDtypeStruct((n, d), F32),
            jax.ShapeDtypeStruct((bsz, HIST, SC_W), F32),
        ],
        scratch_shapes=[pltpu.VMEM((HIST, SC_W), F32)],
        compiler_params=_params(2),
        name="merge",
    )(x2d, z, z, z, z, z, z, ya, yc, hist, sc_w, w_br, w_o)


def _ffn_kernel(x_ref, g2_ref, gf_ref, hist_ref, fw_ref, wup_ref, wdn_ref, y_ref, fs_ref, hbuf,
                *, tm, d_ff, chunk, group, final_norm):
    _init_history(hbuf, hist_ref)
    x = x_ref[...]
    h2 = _rms(x, g2_ref[...]).astype(BF16)
    n_chunks = d_ff // chunk

    def up_proj(i):
        lo = slice(i * chunk, (i + 1) * chunk)
        hi = slice(d_ff + i * chunk, d_ff + (i + 1) * chunk)
        return (lo, _dot(h2, wup_ref[:, lo])), (hi, _dot(h2, wup_ref[:, hi]))

    def conv(cols, up):
        out = _causal_conv3(up, hbuf[:, cols], fw_ref[:, cols])
        hbuf[:, cols] = up[tm - HIST:tm, :]
        return out

    ahead = up_proj(0)
    acc = None
    pieces = []
    for i in range(n_chunks):
        (lo, ua), (hi, ub) = ahead
        if i + 1 < n_chunks:
            ahead = up_proj(i + 1)
        a = conv(lo, ua)
        b = conv(hi, ub)
        pieces.append((a * _sigmoid(a) * b).astype(BF16))
        if len(pieces) == group or i + 1 == n_chunks:
            k0 = (i + 1 - len(pieces)) * chunk
            part = _dot(jnp.concatenate(pieces, axis=1), wdn_ref[k0:(i + 1) * chunk, :])
            acc = part if acc is None else acc + part
            pieces = []
    fs_ref[0] = hbuf[...]
    y = x + acc
    if final_norm:
        y = _rms(y, gf_ref[...])
    y_ref[...] = y


def _ffn(x2d, norm2, norm_f, hist, w_fconv, w_up, w_down, layer, bsz, seq, tm, final_norm):
    n, d = x2d.shape
    d_ff = w_down.shape[1]
    nt = seq // tm
    chunk = 256 if d_ff % 256 == 0 else LANES
    row = lambda b, j: (b * nt + j, 0)
    lay3 = lambda b, j: (layer, 0, 0)
    state = lambda b, j: (b, 0, 0)
    return pl.pallas_call(
        functools.partial(_ffn_kernel, tm=tm, d_ff=d_ff, chunk=chunk, group=FFN_DOWN_GROUP,
                          final_norm=final_norm),
        grid=(bsz, nt),
        in_specs=[
            pl.BlockSpec((tm, d), row),
            pl.BlockSpec((None, 1, d), lay3),
            pl.BlockSpec((1, d), lambda b, j: (0, 0)),
            pl.BlockSpec((1, HIST, 2 * d_ff), state),
            pl.BlockSpec((None, CONV_W, 2 * d_ff), lay3),
            _resident((None, d, 2 * d_ff), lay3),
            _resident((None, d_ff, d), lay3),
        ],
        out_specs=[
            pl.BlockSpec((tm, d), row),
            pl.BlockSpec((1, HIST, 2 * d_ff), state),
        ],
        out_shape=[
            jax.ShapeDtypeStruct((n, d), F32),
            jax.ShapeDtypeStruct((bsz, HIST, 2 * d_ff), F32),
        ],
        scratch_shapes=[pltpu.VMEM((HIST, 2 * d_ff), F32)],
        compiler_params=_params(2),
        name="ffn",
    )(x2d, norm2, norm_f, hist, w_fconv, w_up, w_down)


def _tile(seq, target):
    return target if seq % target == 0 else seq


def _group_layer(x2d, layer, depth, bsz, seq, wts, past, kv_bufs, final_norm):
    z, k_buf, v_buf, gate, gate_t = _in_proj(x2d, wts["norm1"], wts["w_head"], wts["w_tail"], wts["w_gate"],
                                             wts["w_gate_t"],
                                             kv_bufs, layer, depth, _tile(bsz * seq, 512))
    if past is None:
        c0 = jnp.zeros((bsz, ML_HEADS, ML_D, ML_D), F32)
        n0 = jnp.zeros((bsz, ML_HEADS, ML_D), F32)
        m0 = jnp.zeros((bsz, ML_HEADS, LANES), F32)
        conv0 = jnp.zeros((bsz, HIST, SC_W), F32)
        ffn0 = jnp.zeros((bsz, HIST, wts["w_fconv"].shape[2]), F32)
    else:
        c0, n0 = past["C"][layer], past["n"][layer]
        m0 = jnp.broadcast_to(past["m"][layer][..., None], (bsz, ML_HEADS, LANES))
        conv0, ffn0 = past["conv"][layer], past["ffn"][layer]
    length = _tile(seq, 256)
    gate_t = gate_t.reshape(GATE_W, bsz * seq // length, length).swapaxes(0, 1)
    ya, c_new, n_new, m_new = _mlstm(z, gate, gate_t, wts["bias_row"], wts["bias_col"], wts["ml_norm"],
                                     c0, n0, m0, layer, bsz, seq, length)
    if past is None:
        yc = _attn_prompt(z, wts["da_lambda"], wts["da_norm"], layer, bsz, seq, 256)
    else:
        yc = _attn_sample(z, past["k"], past["v"], wts["da_lambda"], wts["da_norm"], layer, bsz, seq)
    x1, conv_new = _merge(x2d, z, ya, yc, conv0, wts["w_sc_conv"], wts["w_branch"], wts["w_out"],
                          layer, bsz, seq, _tile(seq, 512))
    x2, ffn_new = _ffn(x1, wts["norm2"], wts["norm_f"], ffn0, wts["w_fconv"], wts["w_up"], wts["w_down"],
                       layer, bsz, seq, _tile(seq, 512), final_norm)
    return x2, (k_buf, v_buf), (c_new, n_new, m_new[:, :, 0], conv_new, ffn_new)


def kernel(x_prompt, x_sample, cache_k, cache_v, state_C, state_n, state_m, state_conv, state_ffn,
           norm1, w_in, b_if, ml_norm, w_sc_conv, da_lambda, da_norm, w_branch, w_out,
           norm2, w_up, w_fconv, w_down, norm_f):
    depth, d, in_w = w_in.shape
    assert in_w == Z_W + GATE_W and d == 1024
    bp, sp, _ = x_prompt.shape
    bs, ss, _ = x_sample.shape
    past_len = cache_k.shape[2]
    g0 = 4 * HEAD_W

    w_gate = jnp.pad(w_in[:, :, g0:g0 + GATE_W], ((0, 0), (0, 0), (0, LANES - GATE_W))).astype(BF16)
    bias = jnp.concatenate([b_if[:, 0], b_if[:, 1]], axis=-1).astype(F32)
    wts = {
        "norm1": norm1.reshape(depth, 1, d),
        "w_head": w_in[:, :, :g0].astype(BF16),
        "w_tail": w_in[:, :, g0 + GATE_W:].astype(BF16),
        "w_gate": w_gate,
        "w_gate_t": jnp.swapaxes(w_gate, 1, 2),
        "bias_row": jnp.pad(bias, ((0, 0), (0, LANES - GATE_W))).reshape(depth, 1, LANES),
        "bias_col": bias.reshape(depth, GATE_W, 1),
        "ml_norm": ml_norm,
        "w_sc_conv": w_sc_conv,
        "da_lambda": da_lambda,
        "da_norm": da_norm.reshape(depth, 1, DA_DV),
        "w_branch": w_branch.astype(BF16),
        "w_out": w_out.astype(BF16),
        "norm2": norm2.reshape(depth, 1, d),
        "w_up": w_up.astype(BF16),
        "w_fconv": w_fconv,
        "w_down": w_down.astype(BF16),
        "norm_f": norm_f.reshape(1, d),
    }
    past = {
        "k": cache_k.reshape(depth, bs, past_len * DA_HEADS, 2 * DA_DQK),
        "v": cache_v.reshape(depth, bs, past_len * DA_HEADS, DA_DV),
        "C": state_C, "n": state_n, "m": state_m, "conv": state_conv, "ffn": state_ffn,
    }

    xp = x_prompt.reshape(bp * sp, d)
    xs = x_sample.reshape(bs * ss, d)
    p_states, s_states = [], []
    p_kv = s_kv = None
    for layer in range(depth):
        last = layer == depth - 1
        xp, p_kv, st = _group_layer(xp, layer, depth, bp, sp, wts, None, p_kv, last)
        p_states.append(st)
        xs, s_kv, st = _group_layer(xs, layer, depth, bs, ss, wts, past, s_kv, last)
        s_states.append(st)
    kv_shape = lambda b, s: (depth, b, s, DA_HEADS, DA_DV)
    p_out = [t.reshape(kv_shape(bp, sp)) for t in p_kv] + [jnp.stack(t) for t in zip(*p_states)]
    s_out = [t.reshape(kv_shape(bs, ss)) for t in s_kv] + [jnp.stack(t) for t in zip(*s_states)]
    return (xp.reshape(bp, sp, d), xs.reshape(bs, ss, d), *p_out, *s_out)
```

```python
import functools
import itertools
import math

import jax
import jax.numpy as jnp
from jax import lax
from jax.experimental import pallas as pl
from jax.experimental.pallas import tpu as pltpu

F32 = jnp.float32
BF16 = jnp.bfloat16

EPS = 1e-6
ML_HEADS = 4
ML_D = 128
SC_W = 512
DA_HEADS = 4
DA_DQK = 64
DA_DV = 128
ATT_CHUNK = 64
CONV_W = 3
HIST = CONV_W - 1
FFN_DOWN_GROUP = 4
GATE_W = 2 * ML_HEADS
LANES = 128
HEAD_W = ML_HEADS * ML_D
Z_W = 10 * HEAD_W + 3 * 1024
LN_K_SCALE = -0.5 * math.log(ML_D)
Q_SCALE = DA_DQK ** -0.5 * math.log2(math.e)
VMEM_LIMIT = 56 * 2 ** 20

Z_MQ, Z_MK, Z_MV, Z_MO, Z_SB, Z_SC, Z_SX, Z_DQ, Z_DK, Z_DV = range(10)
Z_GT = 5


def _params(n_axes):
    return pltpu.CompilerParams(dimension_semantics=("arbitrary",) * n_axes,
                                vmem_limit_bytes=VMEM_LIMIT)


def _resident(block_shape, index_map):
    return pl.BlockSpec(block_shape, index_map, pipeline_mode=pl.Buffered(1))


def _rms(x, g):
    ms = jnp.mean(x * x, axis=-1, keepdims=True)
    return x * lax.rsqrt(ms + EPS) * g


def _sigmoid(x):
    return 0.5 * jnp.tanh(0.5 * x) + 0.5


def _dot(a, b):
    return jnp.dot(a, b, preferred_element_type=F32)


def _dot_nt(a, b):
    return lax.dot_general(a, b, (((1,), (1,)), ((), ())), preferred_element_type=F32)


def _dot_tn(a, b):
    return lax.dot_general(a, b, (((0,), (0,)), ((), ())), preferred_element_type=F32)


def _inproj_kernel(x_ref, g_ref, wa_ref, wb_ref, wg_ref, wgt_ref, *rest, tm, first):
    z_ref, k_ref, v_ref, gate_ref, gatet_ref = rest if first else rest[2:]
    h = _rms(x_ref[...], g_ref[...]).astype(BF16)
    for c in range(Z_W // HEAD_W):
        cols = slice(c * HEAD_W, (c + 1) * HEAD_W)
        w = wa_ref[:, cols] if c < Z_SB else wb_ref[:, (c - Z_SB) * HEAD_W:(c - Z_SB + 1) * HEAD_W]
        zc = _dot(h, w)
        if c == Z_DQ:
            zc = zc * Q_SCALE
        z_ref[:, cols] = zc.astype(BF16)
        if c == Z_DK:
            _store_heads(k_ref, zc, tm, first)
        if c == Z_DV:
            _store_heads(v_ref, zc, tm, first)
    gate_ref[...] = _dot(h, wg_ref[...])
    gatet_ref[...] = _dot_nt(wgt_ref[...], h)[:GATE_W, :]


def _store_heads(ref, slab, tm, first):
    dst = ref.at[0] if first else ref
    for h in range(DA_HEADS):
        dst[pl.ds(h, tm, stride=DA_HEADS), :] = slab[:, h * DA_DV:(h + 1) * DA_DV]
    if first:
        ref[1:] = jnp.zeros((ref.shape[0] - 1,) + ref.shape[1:], F32)


def _in_proj(x2d, norm1, w_head, w_tail, w_gate, w_gate_t, kv_bufs, layer, depth, tm):
    n, d = x2d.shape
    first = kv_bufs is None
    row = lambda i: (i, 0)
    lay3 = lambda i: (layer, 0, 0)
    if first:
        kv_spec = pl.BlockSpec((depth, tm * DA_HEADS, DA_DV), lambda i: (0, i, 0))
        extra_specs, extra_args, aliases = [], (), {}
    else:
        kv_spec = pl.BlockSpec((None, tm * DA_HEADS, DA_DV), lambda i: (layer, i, 0))
        extra_specs = [pl.BlockSpec(memory_space=pl.ANY)] * 2
        extra_args, aliases = tuple(kv_bufs), {6: 1, 7: 2}
    kv_shape = jax.ShapeDtypeStruct((depth, n * DA_HEADS, DA_DV), F32)
    return pl.pallas_call(
        functools.partial(_inproj_kernel, tm=tm, first=first),
        grid=(n // tm,),
        in_specs=[
            pl.BlockSpec((tm, d), row),
            pl.BlockSpec((None, 1, d), lay3),
            _resident((None, d, Z_SB * HEAD_W), lay3),
            _resident((None, d, Z_W - Z_SB * HEAD_W), lay3),
            _resident((None, d, LANES), lay3),
            _resident((None, LANES, d), lay3),
        ] + extra_specs,
        out_specs=[
            pl.BlockSpec((tm, Z_W), row),
            kv_spec,
            kv_spec,
            pl.BlockSpec((tm, LANES), row),
            pl.BlockSpec((GATE_W, tm), lambda i: (0, i)),
        ],
        out_shape=[
            jax.ShapeDtypeStruct((n, Z_W), BF16),
            kv_shape,
            kv_shape,
            jax.ShapeDtypeStruct((n, LANES), F32),
            jax.ShapeDtypeStruct((GATE_W, n), F32),
        ],
        input_output_aliases=aliases,
        compiler_params=_params(1),
        name="in_proj",
    )(x2d, norm1, w_head, w_tail, w_gate, w_gate_t, *extra_args)


def _log_sigmoid(x):
    return jnp.minimum(x, 0.0) - jnp.log1p(jnp.exp(-jnp.abs(x)))


def _split_bf16(x):
    hi = x.astype(BF16)
    lo = (x - hi.astype(F32)).astype(BF16)
    return hi, lo


def _mlstm_kernel(q_ref, k_ref, v_ref, o_ref, gate_ref, gatet_ref, brow_ref, bcol_ref, mlg_ref,
                  c0_ref, n0_ref, m0_ref, ya_ref, c_ref, n_ref, m_ref, *, length):
    L = length

    @pl.when(pl.program_id(1) == 0)
    def _():
        c_ref[...] = c0_ref[...]
        n_ref[...] = n0_ref[...]
        m_ref[...] = m0_ref[...]

    rows = lax.broadcasted_iota(jnp.int32, (L, L), 0)
    cols = lax.broadcasted_iota(jnp.int32, (L, L), 1)
    tri = jnp.where(rows >= cols, 1.0, 0.0).astype(BF16)
    visible = rows <= cols

    pre_c = gate_ref[...] + brow_ref[...]
    pre_r = gatet_ref[...] + bcol_ref[...]
    hi, lo = _split_bf16(_log_sigmoid(pre_c))
    cum_c = _dot(tri, hi) + _dot(tri, lo)
    hi, lo = _split_bf16(_log_sigmoid(pre_r))
    cum_r = _dot_nt(hi, tri) + _dot_nt(lo, tri)

    heads = range(ML_HEADS)
    hs = [slice(h * ML_D, (h + 1) * ML_D) for h in heads]
    q = [q_ref[:, hs[h]] for h in heads]
    k = [k_ref[:, hs[h]] for h in heads]
    v = [v_ref[:, hs[h]] for h in heads]
    b_row = [cum_r[ML_HEADS + h:ML_HEADS + h + 1, :] for h in heads]
    c_key = [pre_c[:, h:h + 1] - cum_c[:, ML_HEADS + h:ML_HEADS + h + 1] for h in heads]
    m_prev = [m_ref[0, h:h + 1, 0:1] for h in heads]
    c_prev = [c_ref[0, h] for h in heads]
    n_prev = [n_ref[0, h:h + 1, :] for h in heads]

    d = [jnp.where(visible, c_key[h], -jnp.inf) for h in heads]
    mx = [jnp.maximum(jnp.max(d[h], axis=0, keepdims=True), m_prev[h]) for h in heads]
    w = [jnp.exp(d[h] + LN_K_SCALE - mx[h]) for h in heads]
    inter = [jnp.exp(m_prev[h] - mx[h]) for h in heads]
    s = [_dot_nt(k[h], q[h]) * w[h] for h in heads]
    carried = [lax.dot_general(c_prev[h].astype(BF16), q[h], (((0,), (1,)), ((), ())),
                               preferred_element_type=F32) for h in heads]
    num = [_dot_tn(v[h], s[h].astype(BF16)) + inter[h] * carried[h] for h in heads]
    qn = [_dot_nt(n_prev[h].astype(BF16), q[h]) for h in heads]
    den = [jnp.sum(s[h], axis=0, keepdims=True) + inter[h] * qn[h] for h in heads]
    den = [jnp.maximum(jnp.abs(den[h]), jnp.exp(-(b_row[h] + mx[h]))) for h in heads]
    hid = [num[h] / den[h] for h in heads]
    hid = [hid[h] * lax.rsqrt(jnp.mean(hid[h] * hid[h], axis=0, keepdims=True) + EPS) for h in heads]
    for h in heads:
        gain = _sigmoid(o_ref[:, hs[h]].astype(F32)) * mlg_ref[h:h + 1, :]
        ya_ref[:, hs[h]] = (hid[h].T * gain).astype(BF16)

    mx_last = [mx[h][:, L - 1:L] for h in heads]
    il = [jnp.exp(m_prev[h] - mx_last[h]) for h in heads]
    kw = [k[h].astype(F32) * jnp.exp(c_key[h] + LN_K_SCALE - mx_last[h]) for h in heads]
    for h in heads:
        c_ref[0, h] = il[h] * c_prev[h] + _dot_tn(kw[h].astype(BF16), v[h])
        n_ref[0, h:h + 1, :] = il[h] * n_prev[h] + jnp.sum(kw[h], axis=0, keepdims=True)
        m_ref[0, h:h + 1, :] = jnp.broadcast_to(b_row[h][:, L - 1:L] + mx_last[h], (1, LANES))


def _mlstm(z, gate, gate_t, bias_row, bias_col, ml_g, c0, n0, m0, layer, bsz, seq, length):
    n = bsz * seq
    nc = seq // length
    row = lambda col: (lambda b, j: (b * nc + j, col))
    lay3 = lambda b, j: (layer, 0, 0)
    state4 = lambda b, j: (b, 0, 0, 0)
    state3 = lambda b, j: (b, 0, 0)
    return pl.pallas_call(
        functools.partial(_mlstm_kernel, length=length),
        grid=(bsz, nc),
        in_specs=[
            pl.BlockSpec((length, HEAD_W), row(Z_MQ)),
            pl.BlockSpec((length, HEAD_W), row(Z_MK)),
            pl.BlockSpec((length, HEAD_W), row(Z_MV)),
            pl.BlockSpec((length, HEAD_W), row(Z_MO)),
            pl.BlockSpec((length, LANES), row(0)),
            pl.BlockSpec((None, GATE_W, length), lambda b, j: (b * nc + j, 0, 0)),
            pl.BlockSpec((None, 1, LANES), lay3),
            pl.BlockSpec((None, GATE_W, 1), lay3),
            pl.BlockSpec((None, ML_HEADS, ML_D), lay3),
            pl.BlockSpec((1, ML_HEADS, ML_D, ML_D), state4),
            pl.BlockSpec((1, ML_HEADS, ML_D), state3),
            pl.BlockSpec((1, ML_HEADS, LANES), state3),
        ],
        out_specs=[
            pl.BlockSpec((length, HEAD_W), row(0)),
            pl.BlockSpec((1, ML_HEADS, ML_D, ML_D), state4),
            pl.BlockSpec((1, ML_HEADS, ML_D), state3),
            pl.BlockSpec((1, ML_HEADS, LANES), state3),
        ],
        out_shape=[
            jax.ShapeDtypeStruct((n, HEAD_W), BF16),
            jax.ShapeDtypeStruct((bsz, ML_HEADS, ML_D, ML_D), F32),
            jax.ShapeDtypeStruct((bsz, ML_HEADS, ML_D), F32),
            jax.ShapeDtypeStruct((bsz, ML_HEADS, LANES), F32),
        ],
        compiler_params=_params(2),
        name="mlstm",
    )(z, z, z, z, gate, gate_t, bias_row, bias_col, ml_g, c0, n0, m0)


def _lambda(lam_ref, lam_init):
    lp = lam_ref[...]
    a = jnp.sum(lp[0:1] * lp[1:2], axis=1, keepdims=True)
    b = jnp.sum(lp[2:3] * lp[3:4], axis=1, keepdims=True)
    return jnp.exp(a) - jnp.exp(b) + lam_init


def _split_q(q):
    lane = lax.broadcasted_iota(jnp.int32, q.shape, 1)
    zero = jnp.zeros_like(q)
    return jnp.where(lane < DA_DQK, q, zero), jnp.where(lane >= DA_DQK, q, zero)


def _attn_prompt_kernel(lam_ref, dag_ref, q_ref, k_ref, v_ref, o_ref, *, blk, nq, lam_init):
    lam = _lambda(lam_ref, lam_init)
    rows = lax.broadcasted_iota(jnp.int32, (2 * blk, blk), 0)
    cols = lax.broadcasted_iota(jnp.int32, (2 * blk, blk), 1)
    mask = (cols // ATT_CHUNK) <= ((rows % blk) // ATT_CHUNK)
    half = blk // 2
    st = [dict(s=[], p=[]) for _ in range(nq)]
    keys = lambda j: slice(j * blk, (j + 1) * blk)

    def lane_fold(x, op):
        return op(x[:, :half], x[:, half:])

    def pass1(c):
        b = st[c]

        def start():
            b["q2"] = jnp.concatenate(_split_q(q_ref[keys(c), :]), axis=0)

        def chunk(j):
            s = _dot_nt(b["q2"], k_ref[keys(j), :])
            if j == c:
                s = jnp.where(mask, s, -jnp.inf)
            mj = lane_fold(s, jnp.maximum)
            b["m"] = mj if j == 0 else jnp.maximum(b["m"], mj)
            b["s"].append(s)

        def finish():
            b["m"] = jnp.max(b["m"], axis=1, keepdims=True)

        return [start] + [functools.partial(chunk, j) for j in range(c + 1)] + [finish]

    def pass2(c):
        b = st[c]

        def chunk(j):
            p = jnp.exp2(b["s"][j] - b["m"])
            lj = lane_fold(p, jnp.add)
            b["l"] = lj if j == 0 else b["l"] + lj
            b["p"].append(p)

        def finish():
            l = jnp.sum(b["l"], axis=1, keepdims=True)
            b["l0"] = l[:blk]
            b["rho"] = lam * l[:blk] / l[blk:]

        return [functools.partial(chunk, j) for j in range(c + 1)] + [finish]

    def pass3(c):
        b = st[c]

        def chunk(j):
            p = b["p"][j]
            part = _dot((p[:blk] - b["rho"] * p[blk:]).astype(BF16), v_ref[keys(j), :])
            b["o"] = part if j == 0 else b["o"] + part

        def finish():
            o = b["o"] / b["l0"]
            o_ref[keys(c), :] = (_rms(o, dag_ref[...]) * (1.0 - lam_init)).astype(BF16)
            b.clear()

        return [functools.partial(chunk, j) for j in range(c + 1)] + [finish]

    for stage in range(nq + 2):
        active = [make(stage - lag) for lag, make in enumerate((pass1, pass2, pass3)) if 0 <= stage - lag < nq]
        for group in itertools.zip_longest(*active):
            for thunk in group:
                if thunk is not None:
                    thunk()


def _attn_prompt(z, da_lambda, da_norm, layer, bsz, seq, blk):
    nq = seq // blk
    lam_init = 0.8 - 0.6 * math.exp(-0.3 * layer)
    lay3 = lambda b, h: (layer, 0, 0)
    whole = lambda slab: pl.BlockSpec((seq, DA_DV), lambda b, h: (b, slab * DA_HEADS + h))
    return pl.pallas_call(
        functools.partial(_attn_prompt_kernel, blk=blk, nq=nq, lam_init=lam_init),
        grid=(bsz, DA_HEADS),
        in_specs=[
            pl.BlockSpec((None, 4, DA_DQK), lay3),
            pl.BlockSpec((None, 1, DA_DV), lay3),
            whole(Z_DQ), whole(Z_DK), whole(Z_DV),
        ],
        out_specs=pl.BlockSpec((seq, DA_DV), lambda b, h: (b, h)),
        out_shape=jax.ShapeDtypeStruct((bsz * seq, HEAD_W), BF16),
        compiler_params=_params(2),
        name="attn_prompt",
    )(da_lambda, da_norm, z, z, z)


def _attn_sample_kernel(lam_ref, dag_ref, q_ref, kn_ref, vn_ref, kp_ref, vp_ref, o_ref, *, past, lam_init):
    lam = _lambda(lam_ref, lam_init)
    t = q_ref.shape[0]
    for h in range(DA_HEADS):
        hs = slice(h * DA_DV, (h + 1) * DA_DV)
        q2 = jnp.concatenate(_split_q(q_ref[:, hs]), axis=0)
        kp = kp_ref[pl.ds(h, past, stride=DA_HEADS), :].astype(BF16)
        vp = vp_ref[pl.ds(h, past, stride=DA_HEADS), :].astype(BF16)
        sp = _dot_nt(q2, kp)
        sn = _dot_nt(q2, kn_ref[:, hs])
        m = jnp.maximum(jnp.max(sp, axis=1, keepdims=True), jnp.max(sn, axis=1, keepdims=True))
        pp = jnp.exp2(sp - m)
        pn = jnp.exp2(sn - m)
        l = jnp.sum(pp, axis=1, keepdims=True) + jnp.sum(pn, axis=1, keepdims=True)
        rho = lam * l[:t] / l[t:]
        o = (_dot((pp[:t] - rho * pp[t:]).astype(BF16), vp)
             + _dot((pn[:t] - rho * pn[t:]).astype(BF16), vn_ref[:, hs]))
        o = o / l[:t]
        o_ref[:, hs] = (_rms(o, dag_ref[...]) * (1.0 - lam_init)).astype(BF16)


def _attn_sample(z, cache_k, cache_v, da_lambda, da_norm, layer, bsz, seq):
    past = cache_k.shape[2] // DA_HEADS
    lam_init = 0.8 - 0.6 * math.exp(-0.3 * layer)
    lay3 = lambda b: (layer, 0, 0)
    cache = pl.BlockSpec((None, None, past * DA_HEADS, DA_DV), lambda b: (layer, b, 0, 0))
    slab = lambda col: pl.BlockSpec((seq, HEAD_W), lambda b: (b, col))
    return pl.pallas_call(
        functools.partial(_attn_sample_kernel, past=past, lam_init=lam_init),
        grid=(bsz,),
        in_specs=[
            pl.BlockSpec((None, 4, DA_DQK), lay3),
            pl.BlockSpec((None, 1, DA_DV), lay3),
            slab(Z_DQ), slab(Z_DK), slab(Z_DV),
            cache, cache,
        ],
        out_specs=pl.BlockSpec((seq, HEAD_W), lambda b: (b, 0)),
        out_shape=jax.ShapeDtypeStruct((bsz * seq, HEAD_W), BF16),
        compiler_params=_params(1),
        name="attn_sample",
    )(da_lambda, da_norm, z, z, z, cache_k, cache_v)


def _causal_conv3(u, hist, w):
    row = lax.broadcasted_iota(jnp.int32, (8, u.shape[1]), 0)
    r1 = pltpu.roll(u, 1, 0)
    r2 = pltpu.roll(u, 2, 0)
    top1 = jnp.where(row == 0, hist[1:2], r1[:8])
    top2 = jnp.where(row == 0, hist[0:1], jnp.where(row == 1, hist[1:2], r2[:8]))
    x1 = jnp.concatenate([top1, r1[8:]], axis=0)
    x2 = jnp.concatenate([top2, r2[8:]], axis=0)
    return w[2:3] * u + w[1:2] * x1 + w[0:1] * x2


def _init_history(hbuf, hist_ref):
    @pl.when(pl.program_id(1) == 0)
    def _():
        hbuf[...] = hist_ref[0]


def _merge_kernel(x_ref, sb_ref, sc_ref, sx_ref, g0_ref, g1_ref, g2_ref, ya_ref, yc_ref, hist_ref,
                  scw_ref, wbr_ref, wo_ref, x1_ref, cs_ref, hbuf, *, tm):
    _init_history(hbuf, hist_ref)
    u = sc_ref[...].astype(F32) * sx_ref[...].astype(F32)
    cu = _causal_conv3(u, hbuf[...], scw_ref[...])
    hbuf[...] = u[tm - HIST:tm, :]
    cs_ref[0] = u[tm - HIST:tm, :]
    yb = (sb_ref[...].astype(F32) * cu).astype(BF16)
    mix = (_sigmoid(g0_ref[...].astype(F32)) * _dot(ya_ref[...], wbr_ref[0])
           + _sigmoid(g1_ref[...].astype(F32)) * _dot(yb, wbr_ref[1])
           + _sigmoid(g2_ref[...].astype(F32)) * _dot(yc_ref[...], wbr_ref[2]))
    x1_ref[...] = x_ref[...] + _dot(mix.astype(BF16), wo_ref[...])


def _merge(x2d, z, ya, yc, hist, sc_w, w_br, w_o, layer, bsz, seq, tm):
    n, d = x2d.shape
    nt = seq // tm
    row = lambda col: (lambda b, j: (b * nt + j, col))
    lay3 = lambda b, j: (layer, 0, 0)
    state = lambda b, j: (b, 0, 0)
    return pl.pallas_call(
        functools.partial(_merge_kernel, tm=tm),
        grid=(bsz, nt),
        in_specs=[
            pl.BlockSpec((tm, d), row(0)),
            pl.BlockSpec((tm, HEAD_W), row(Z_SB)),
            pl.BlockSpec((tm, HEAD_W), row(Z_SC)),
            pl.BlockSpec((tm, HEAD_W), row(Z_SX)),
            pl.BlockSpec((tm, d), row(Z_GT)),
            pl.BlockSpec((tm, d), row(Z_GT + 1)),
            pl.BlockSpec((tm, d), row(Z_GT + 2)),
            pl.BlockSpec((tm, HEAD_W), row(0)),
            pl.BlockSpec((tm, HEAD_W), row(0)),
            pl.BlockSpec((1, HIST, SC_W), state),
            pl.BlockSpec((None, CONV_W, SC_W), lay3),
            _resident((None, 3, HEAD_W, d), lambda b, j: (layer, 0, 0, 0)),
            _resident((None, d, d), lay3),
        ],
        out_specs=[
            pl.BlockSpec((tm, d), row(0)),
            pl.BlockSpec((1, HIST, SC_W), state),
        ],
        out_shape=[
            jax.ShapeDtypeStruct((n, d), F32),
            jax.ShapeDtypeStruct((bsz, HIST, SC_W), F32),
        ],
        scratch_shapes=[pltpu.VMEM((HIST, SC_W), F32)],
        compiler_params=_params(2),
        name="merge",
    )(x2d, z, z, z, z, z, z, ya, yc, hist, sc_w, w_br, w_o)


def _ffn_kernel(x_ref, g2_ref, gf_ref, hist_ref, fw_ref, wup_ref, wdn_ref, y_ref, fs_ref, hbuf,
                *, tm, d_ff, chunk, group, final_norm):
    _init_history(hbuf, hist_ref)
    x = x_ref[...]
    h2 = _rms(x, g2_ref[...]).astype(BF16)
    n_chunks = d_ff // chunk

    def up_proj(i):
        lo = slice(i * chunk, (i + 1) * chunk)
        hi = slice(d_ff + i * chunk, d_ff + (i + 1) * chunk)
        return (lo, _dot(h2, wup_ref[:, lo])), (hi, _dot(h2, wup_ref[:, hi]))

    def conv(cols, up):
        out = _causal_conv3(up, hbuf[:, cols], fw_ref[:, cols])
        hbuf[:, cols] = up[tm - HIST:tm, :]
        return out

    ahead = up_proj(0)
    acc = None
    pieces = []
    for i in range(n_chunks):
        (lo, ua), (hi, ub) = ahead
        if i + 1 < n_chunks:
            ahead = up_proj(i + 1)
        a = conv(lo, ua)
        b = conv(hi, ub)
        pieces.append((a * _sigmoid(a) * b).astype(BF16))
        if len(pieces) == group or i + 1 == n_chunks:
            k0 = (i + 1 - len(pieces)) * chunk
            part = _dot(jnp.concatenate(pieces, axis=1), wdn_ref[k0:(i + 1) * chunk, :])
            acc = part if acc is None else acc + part
            pieces = []
    fs_ref[0] = hbuf[...]
    y = x + acc
    if final_norm:
        y = _rms(y, gf_ref[...])
    y_ref[...] = y


def _ffn(x2d, norm2, norm_f, hist, w_fconv, w_up, w_down, layer, bsz, seq, tm, final_norm):
    n, d = x2d.shape
    d_ff = w_down.shape[1]
    nt = seq // tm
    chunk = 256 if d_ff % 256 == 0 else LANES
    row = lambda b, j: (b * nt + j, 0)
    lay3 = lambda b, j: (layer, 0, 0)
    state = lambda b, j: (b, 0, 0)
    return pl.pallas_call(
        functools.partial(_ffn_kernel, tm=tm, d_ff=d_ff, chunk=chunk, group=FFN_DOWN_GROUP,
                          final_norm=final_norm),
        grid=(bsz, nt),
        in_specs=[
            pl.BlockSpec((tm, d), row),
            pl.BlockSpec((None, 1, d), lay3),
            pl.BlockSpec((1, d), lambda b, j: (0, 0)),
            pl.BlockSpec((1, HIST, 2 * d_ff), state),
            pl.BlockSpec((None, CONV_W, 2 * d_ff), lay3),
            _resident((None, d, 2 * d_ff), lay3),
            _resident((None, d_ff, d), lay3),
        ],
        out_specs=[
            pl.BlockSpec((tm, d), row),
            pl.BlockSpec((1, HIST, 2 * d_ff), state),
        ],
        out_shape=[
            jax.ShapeDtypeStruct((n, d), F32),
            jax.ShapeDtypeStruct((bsz, HIST, 2 * d_ff), F32),
        ],
        scratch_shapes=[pltpu.VMEM((HIST, 2 * d_ff), F32)],
        compiler_params=_params(2),
        name="ffn",
    )(x2d, norm2, norm_f, hist, w_fconv, w_up, w_down)


def _tile(seq, target):
    return target if seq % target == 0 else seq


def _group_layer(x2d, layer, depth, bsz, seq, wts, past, kv_bufs, final_norm):
    z, k_buf, v_buf, gate, gate_t = _in_proj(x2d, wts["norm1"], wts["w_head"], wts["w_tail"], wts["w_gate"],
                                             wts["w_gate_t"],
                                             kv_bufs, layer, depth, _tile(bsz * seq, 512))
    if past is None:
        c0 = jnp.zeros((bsz, ML_HEADS, ML_D, ML_D), F32)
        n0 = jnp.zeros((bsz, ML_HEADS, ML_D), F32)
        m0 = jnp.zeros((bsz, ML_HEADS, LANES), F32)
        conv0 = jnp.zeros((bsz, HIST, SC_W), F32)
        ffn0 = jnp.zeros((bsz, HIST, wts["w_fconv"].shape[2]), F32)
    else:
        c0, n0 = past["C"][layer], past["n"][layer]
        m0 = jnp.broadcast_to(past["m"][layer][..., None], (bsz, ML_HEADS, LANES))
        conv0, ffn0 = past["conv"][layer], past["ffn"][layer]
    length = _tile(seq, 256)
    gate_t = gate_t.reshape(GATE_W, bsz * seq // length, length).swapaxes(0, 1)
    ya, c_new, n_new, m_new = _mlstm(z, gate, gate_t, wts["bias_row"], wts["bias_col"], wts["ml_norm"],
                                     c0, n0, m0, layer, bsz, seq, length)
    if past is None:
        yc = _attn_prompt(z, wts["da_lambda"], wts["da_norm"], layer, bsz, seq, 256)
    else:
        yc = _attn_sample(z, past["k"], past["v"], wts["da_lambda"], wts["da_norm"], layer, bsz, seq)
    x1, conv_new = _merge(x2d, z, ya, yc, conv0, wts["w_sc_conv"], wts["w_branch"], wts["w_out"],
                          layer, bsz, seq, _tile(seq, 512))
    x2, ffn_new = _ffn(x1, wts["norm2"], wts["norm_f"], ffn0, wts["w_fconv"], wts["w_up"], wts["w_down"],
                       layer, bsz, seq, _tile(seq, 512), final_norm)
    return x2, (k_buf, v_buf), (c_new, n_new, m_new[:, :, 0], conv_new, ffn_new)


def kernel(x_prompt, x_sample, cache_k, cache_v, state_C, state_n, state_m, state_conv, state_ffn,
           norm1, w_in, b_if, ml_norm, w_sc_conv, da_lambda, da_norm, w_branch, w_out,
           norm2, w_up, w_fconv, w_down, norm_f):
    depth, d, in_w = w_in.shape
    assert in_w == Z_W + GATE_W and d == 1024
    bp, sp, _ = x_prompt.shape
    bs, ss, _ = x_sample.shape
    past_len = cache_k.shape[2]
    g0 = 4 * HEAD_W

    w_gate = jnp.pad(w_in[:, :, g0:g0 + GATE_W], ((0, 0), (0, 0), (0, LANES - GATE_W))).astype(BF16)
    bias = jnp.concatenate([b_if[:, 0], b_if[:, 1]], axis=-1).astype(F32)
    wts = {
        "norm1": norm1.reshape(depth, 1, d),
        "w_head": w_in[:, :, :g0].astype(BF16),
        "w_tail": w_in[:, :, g0 + GATE_W:].astype(BF16),
        "w_gate": w_gate,
        "w_gate_t": jnp.swapaxes(w_gate, 1, 2),
        "bias_row": jnp.pad(bias, ((0, 0), (0, LANES - GATE_W))).reshape(depth, 1, LANES),
        "bias_col": bias.reshape(depth, GATE_W, 1),
        "ml_norm": ml_norm,
        "w_sc_conv": w_sc_conv,
        "da_lambda": da_lambda,
        "da_norm": da_norm.reshape(depth, 1, DA_DV),
        "w_branch": w_branch.astype(BF16),
        "w_out": w_out.astype(BF16),
        "norm2": norm2.reshape(depth, 1, d),
        "w_up": w_up.astype(BF16),
        "w_fconv": w_fconv,
        "w_down": w_down.astype(BF16),
        "norm_f": norm_f.reshape(1, d),
    }
    past = {
        "k": cache_k.reshape(depth, bs, past_len * DA_HEADS, 2 * DA_DQK),
        "v": cache_v.reshape(depth, bs, past_len * DA_HEADS, DA_DV),
        "C": state_C, "n": state_n, "m": state_m, "conv": state_conv, "ffn": state_ffn,
    }

    xp = x_prompt.reshape(bp * sp, d)
    xs = x_sample.reshape(bs * ss, d)
    p_states, s_states = [], []
    p_kv = s_kv = None
    for layer in range(depth):
        last = layer == depth - 1
        xp, p_kv, st = _group_layer(xp, layer, depth, bp, sp, wts, None, p_kv, last)
        p_states.append(st)
        xs, s_kv, st = _group_layer(xs, layer, depth, bs, ss, wts, past, s_kv, last)
        s_states.append(st)
    kv_shape = lambda b, s: (depth, b, s, DA_HEADS, DA_DV)
    p_out = [t.reshape(kv_shape(bp, sp)) for t in p_kv] + [jnp.stack(t) for t in zip(*p_states)]
    s_out = [t.reshape(kv_shape(bs, ss)) for t in s_kv] + [jnp.stack(t) for t in zip(*s_states)]
    return (xp.reshape(bp, sp, d), xs.reshape(bs, ss, d), *p_out, *s_out)
```

```python
import functools
import itertools
import math

import jax
import jax.numpy as jnp
from jax import lax
from jax.experimental import pallas as pl
from jax.experimental.pallas import tpu as pltpu

F32 = jnp.float32
BF16 = jnp.bfloat16

EPS = 1e-6
ML_HEADS = 4
ML_D = 128
SC_W = 512
DA_HEADS = 4
DA_DQK = 64
DA_DV = 128
ATT_CHUNK = 64
CONV_W = 3
HIST = CONV_W - 1
FFN_DOWN_GROUP = 4
GATE_W = 2 * ML_HEADS
LANES = 128
HEAD_W = ML_HEADS * ML_D
Z_W = 10 * HEAD_W + 3 * 1024
LN_K_SCALE = -0.5 * math.log(ML_D)
Q_SCALE = DA_DQK ** -0.5 * math.log2(math.e)
VMEM_LIMIT = 56 * 2 ** 20

Z_MQ, Z_MK, Z_MV, Z_MO, Z_SB, Z_SC, Z_SX, Z_DQ, Z_DK, Z_DV = range(10)
Z_GT = 5


def _params(n_axes):
    return pltpu.CompilerParams(dimension_semantics=("arbitrary",) * n_axes,
                                vmem_limit_bytes=VMEM_LIMIT)


def _resident(block_shape, index_map):
    return pl.BlockSpec(block_shape, index_map, pipeline_mode=pl.Buffered(1))


def _rms(x, g):
    ms = jnp.mean(x * x, axis=-1, keepdims=True)
    return x * lax.rsqrt(ms + EPS) * g


def _sigmoid(x):
    return 0.5 * jnp.tanh(0.5 * x) + 0.5


def _dot(a, b):
    return jnp.dot(a, b, preferred_element_type=F32)


def _dot_nt(a, b):
    return lax.dot_general(a, b, (((1,), (1,)), ((), ())), preferred_element_type=F32)


def _dot_tn(a, b):
    return lax.dot_general(a, b, (((0,), (0,)), ((), ())), preferred_element_type=F32)


def _inproj_kernel(x_ref, g_ref, wa_ref, wb_ref, wg_ref, wgt_ref, *rest, tm, first, scan):
    rest = list(rest)
    if scan:
        brow_ref, bcol_ref, mlg_ref, c0_ref, n0_ref, m0_ref = rest[:6]
        rest = rest[6:]
    if not first:
        rest = rest[2:]
    if scan:
        z_ref, k_ref, v_ref, ya_ref, c_ref, n_ref, m_ref = rest
    else:
        z_ref, k_ref, v_ref, gate_ref, gatet_ref = rest
    h = _rms(x_ref[...], g_ref[...]).astype(BF16)

    def project(c):
        cols = slice(c * HEAD_W, (c + 1) * HEAD_W)
        w = wa_ref[:, cols] if c < Z_SB else wb_ref[:, (c - Z_SB) * HEAD_W:(c - Z_SB + 1) * HEAD_W]
        zc = _dot(h, w)
        if c == Z_DQ:
            zc = zc * Q_SCALE
        z_ref[:, cols] = zc.astype(BF16)
        if c == Z_DK:
            _store_heads(k_ref, zc, tm, first)
        if c == Z_DV:
            _store_heads(v_ref, zc, tm, first)

    scan_slabs = (Z_MQ, Z_MK, Z_MV, Z_MO)
    for c in scan_slabs:
        project(c)
    gate = _dot(h, wg_ref[...])
    gate_t = _dot_nt(wgt_ref[...], h)[:GATE_W, :]
    others = [c for c in range(Z_W // HEAD_W) if c not in scan_slabs]
    if not scan:
        for c in others:
            project(c)
        gate_ref[...] = gate
        gatet_ref[...] = gate_t
        return

    length, tiles_per_seq = scan

    @pl.when(lax.rem(pl.program_id(0), tiles_per_seq) == 0)
    def _():
        c_ref[...] = c0_ref[...]
        n_ref[...] = n0_ref[...]
        m_ref[...] = m0_ref[...]

    def chunk_stages(start):
        rows = slice(start, start + length)

        def load(slab, hd):
            return z_ref[rows, slab * HEAD_W + hd * ML_D:slab * HEAD_W + (hd + 1) * ML_D]

        def store(hd, val):
            ya_ref[rows, hd * ML_D:(hd + 1) * ML_D] = val

        return _mlstm_stages(load, store, gate[rows, :], gate_t[:, rows], brow_ref, bcol_ref, mlg_ref,
                             c_ref, n_ref, m_ref, length)

    stages = itertools.chain(*[chunk_stages(start) for start in range(0, tm, length)])
    per_chunk = -(-MLSTM_STAGES * (tm // length) // len(others))
    for c in others:
        project(c)
        for _ in itertools.islice(stages, per_chunk):
            pass
    for _ in stages:
        pass


def _store_heads(ref, slab, tm, first):
    dst = ref.at[0] if first else ref
    for h in range(DA_HEADS):
        dst[pl.ds(h, tm, stride=DA_HEADS), :] = slab[:, h * DA_DV:(h + 1) * DA_DV]
    if first:
        ref[1:] = jnp.zeros((ref.shape[0] - 1,) + ref.shape[1:], F32)


def _in_proj(x2d, norm1, w_head, w_tail, w_gate, w_gate_t, kv_bufs, layer, depth, tm, scan=None):
    n, d = x2d.shape
    first = kv_bufs is None
    row = lambda i: (i, 0)
    lay3 = lambda i: (layer, 0, 0)
    scan_specs, scan_args, scan_cfg = [], (), None
    tail_specs = [pl.BlockSpec((tm, LANES), row), pl.BlockSpec((GATE_W, tm), lambda i: (0, i))]
    tail_shapes = [jax.ShapeDtypeStruct((n, LANES), F32), jax.ShapeDtypeStruct((GATE_W, n), F32)]
    if scan is not None:
        *scan_args, seq, length = scan
        tiles = seq // tm
        bsz = n // seq
        scan_cfg = (length, tiles)
        state4 = lambda i: (i // tiles, 0, 0, 0)
        state3 = lambda i: (i // tiles, 0, 0)
        state_specs = [pl.BlockSpec((1, ML_HEADS, ML_D, ML_D), state4),
                       pl.BlockSpec((1, ML_HEADS, ML_D), state3),
                       pl.BlockSpec((1, ML_HEADS, LANES), state3)]
        scan_specs = [pl.BlockSpec((None, 1, LANES), lay3), pl.BlockSpec((None, GATE_W, 1), lay3),
                      pl.BlockSpec((None, ML_HEADS, ML_D), lay3)] + state_specs
        tail_specs = [pl.BlockSpec((tm, HEAD_W), row)] + state_specs
        tail_shapes = [jax.ShapeDtypeStruct((n, HEAD_W), BF16),
                       jax.ShapeDtypeStruct((bsz, ML_HEADS, ML_D, ML_D), F32),
                       jax.ShapeDtypeStruct((bsz, ML_HEADS, ML_D), F32),
                       jax.ShapeDtypeStruct((bsz, ML_HEADS, LANES), F32)]
    if first:
        kv_spec = pl.BlockSpec((depth, tm * DA_HEADS, DA_DV), lambda i: (0, i, 0))
        kv_specs, kv_args, aliases = [], (), {}
    else:
        kv_spec = pl.BlockSpec((None, tm * DA_HEADS, DA_DV), lambda i: (layer, i, 0))
        kv_specs = [pl.BlockSpec(memory_space=pl.ANY)] * 2
        kv_args = tuple(kv_bufs)
        first_kv = 6 + len(scan_specs)
        aliases = {first_kv: 1, first_kv + 1: 2}
    kv_shape = jax.ShapeDtypeStruct((depth, n * DA_HEADS, DA_DV), F32)
    return pl.pallas_call(
        functools.partial(_inproj_kernel, tm=tm, first=first, scan=scan_cfg),
        grid=(n // tm,),
        in_specs=[
            pl.BlockSpec((tm, d), row),
            pl.BlockSpec((None, 1, d), lay3),
            _resident((None, d, Z_SB * HEAD_W), lay3),
            _resident((None, d, Z_W - Z_SB * HEAD_W), lay3),
            _resident((None, d, LANES), lay3),
            _resident((None, LANES, d), lay3),
        ] + scan_specs + kv_specs,
        out_specs=[pl.BlockSpec((tm, Z_W), row), kv_spec, kv_spec] + tail_specs,
        out_shape=[jax.ShapeDtypeStruct((n, Z_W), BF16), kv_shape, kv_shape] + tail_shapes,
        input_output_aliases=aliases,
        compiler_params=_params(1),
        name="in_proj_scan" if scan is not None else "in_proj",
    )(x2d, norm1, w_head, w_tail, w_gate, w_gate_t, *scan_args, *kv_args)


def _log_sigmoid(x):
    return jnp.minimum(x, 0.0) - jnp.log1p(jnp.exp(-jnp.abs(x)))


def _split_bf16(x):
    hi = x.astype(BF16)
    lo = (x - hi.astype(F32)).astype(BF16)
    return hi, lo


MLSTM_STAGES = 16


def _mlstm_stages(load, store, gate, gate_t, brow_ref, bcol_ref, mlg_ref, c_ref, n_ref, m_ref, L):
    rows = lax.broadcasted_iota(jnp.int32, (L, L), 0)
    cols = lax.broadcasted_iota(jnp.int32, (L, L), 1)
    tri = jnp.where(rows >= cols, 1.0, 0.0).astype(BF16)
    visible = rows <= cols

    pre_c = gate + brow_ref[...]
    pre_r = gate_t + bcol_ref[...]
    hi, lo = _split_bf16(_log_sigmoid(pre_c))
    cum_c = _dot(tri, hi) + _dot(tri, lo)
    hi, lo = _split_bf16(_log_sigmoid(pre_r))
    cum_r = _dot_nt(hi, tri) + _dot_nt(lo, tri)
    yield

    heads = range(ML_HEADS)
    q = [load(Z_MQ, h) for h in heads]
    k = [load(Z_MK, h) for h in heads]
    v = [load(Z_MV, h) for h in heads]
    b_row = [cum_r[ML_HEADS + h:ML_HEADS + h + 1, :] for h in heads]
    c_key = [pre_c[:, h:h + 1] - cum_c[:, ML_HEADS + h:ML_HEADS + h + 1] for h in heads]
    m_prev = [m_ref[0, h:h + 1, 0:1] for h in heads]
    c_prev = [c_ref[0, h] for h in heads]
    n_prev = [n_ref[0, h:h + 1, :] for h in heads]
    yield

    d = [jnp.where(visible, c_key[h], -jnp.inf) for h in heads]
    yield
    mx = [jnp.maximum(jnp.max(d[h], axis=0, keepdims=True), m_prev[h]) for h in heads]
    yield
    w = [jnp.exp(d[h] + LN_K_SCALE - mx[h]) for h in heads]
    inter = [jnp.exp(m_prev[h] - mx[h]) for h in heads]
    yield
    s = [_dot_nt(k[h], q[h]) * w[h] for h in heads]
    yield
    carried = [lax.dot_general(c_prev[h].astype(BF16), q[h], (((0,), (1,)), ((), ())),
                               preferred_element_type=F32) for h in heads]
    yield
    num = [_dot_tn(v[h], s[h].astype(BF16)) + inter[h] * carried[h] for h in heads]
    yield
    qn = [_dot_nt(n_prev[h].astype(BF16), q[h]) for h in heads]
    den = [jnp.sum(s[h], axis=0, keepdims=True) + inter[h] * qn[h] for h in heads]
    yield
    den = [jnp.maximum(jnp.abs(den[h]), jnp.exp(-(b_row[h] + mx[h]))) for h in heads]
    hid = [num[h] / den[h] for h in heads]
    yield
    hid = [hid[h] * lax.rsqrt(jnp.mean(hid[h] * hid[h], axis=0, keepdims=True) + EPS) for h in heads]
    yield
    for h in heads:
        gain = _sigmoid(load(Z_MO, h).astype(F32)) * mlg_ref[h:h + 1, :]
        store(h, (hid[h].T * gain).astype(BF16))
    yield

    mx_last = [mx[h][:, L - 1:L] for h in heads]
    il = [jnp.exp(m_prev[h] - mx_last[h]) for h in heads]
    kw = [k[h].astype(F32) * jnp.exp(c_key[h] + LN_K_SCALE - mx_last[h]) for h in heads]
    yield
    kv = [_dot_tn(kw[h].astype(BF16), v[h]) for h in heads]
    yield
    for h in heads:
        c_ref[0, h] = il[h] * c_prev[h] + kv[h]
        n_ref[0, h:h + 1, :] = il[h] * n_prev[h] + jnp.sum(kw[h], axis=0, keepdims=True)
    yield
    for h in heads:
        m_ref[0, h:h + 1, :] = jnp.broadcast_to(b_row[h][:, L - 1:L] + mx_last[h], (1, LANES))
    yield


def _mlstm_kernel(q_ref, k_ref, v_ref, o_ref, gate_ref, gatet_ref, brow_ref, bcol_ref, mlg_ref,
                  c0_ref, n0_ref, m0_ref, ya_ref, c_ref, n_ref, m_ref, *, length):
    @pl.when(pl.program_id(1) == 0)
    def _():
        c_ref[...] = c0_ref[...]
        n_ref[...] = n0_ref[...]
        m_ref[...] = m0_ref[...]

    slabs = {Z_MQ: q_ref, Z_MK: k_ref, Z_MV: v_ref, Z_MO: o_ref}

    def load(slab, h):
        return slabs[slab][:, h * ML_D:(h + 1) * ML_D]

    def store(h, val):
        ya_ref[:, h * ML_D:(h + 1) * ML_D] = val

    for _ in _mlstm_stages(load, store, gate_ref[...], gatet_ref[...], brow_ref, bcol_ref, mlg_ref,
                           c_ref, n_ref, m_ref, length):
        pass


def _mlstm(z, gate, gate_t, bias_row, bias_col, ml_g, c0, n0, m0, layer, bsz, seq, length):
    n = bsz * seq
    nc = seq // length
    row = lambda col: (lambda b, j: (b * nc + j, col))
    lay3 = lambda b, j: (layer, 0, 0)
    state4 = lambda b, j: (b, 0, 0, 0)
    state3 = lambda b, j: (b, 0, 0)
    return pl.pallas_call(
        functools.partial(_mlstm_kernel, length=length),
        grid=(bsz, nc),
        in_specs=[
            pl.BlockSpec((length, HEAD_W), row(Z_MQ)),
            pl.BlockSpec((length, HEAD_W), row(Z_MK)),
            pl.BlockSpec((length, HEAD_W), row(Z_MV)),
            pl.BlockSpec((length, HEAD_W), row(Z_MO)),
            pl.BlockSpec((length, LANES), row(0)),
            pl.BlockSpec((None, GATE_W, length), lambda b, j: (b * nc + j, 0, 0)),
            pl.BlockSpec((None, 1, LANES), lay3),
            pl.BlockSpec((None, GATE_W, 1), lay3),
            pl.BlockSpec((None, ML_HEADS, ML_D), lay3),
            pl.BlockSpec((1, ML_HEADS, ML_D, ML_D), state4),
            pl.BlockSpec((1, ML_HEADS, ML_D), state3),
            pl.BlockSpec((1, ML_HEADS, LANES), state3),
        ],
        out_specs=[
            pl.BlockSpec((length, HEAD_W), row(0)),
            pl.BlockSpec((1, ML_HEADS, ML_D, ML_D), state4),
            pl.BlockSpec((1, ML_HEADS, ML_D), state3),
            pl.BlockSpec((1, ML_HEADS, LANES), state3),
        ],
        out_shape=[
            jax.ShapeDtypeStruct((n, HEAD_W), BF16),
            jax.ShapeDtypeStruct((bsz, ML_HEADS, ML_D, ML_D), F32),
            jax.ShapeDtypeStruct((bsz, ML_HEADS, ML_D), F32),
            jax.ShapeDtypeStruct((bsz, ML_HEADS, LANES), F32),
        ],
        compiler_params=_params(2),
        name="mlstm",
    )(z, z, z, z, gate, gate_t, bias_row, bias_col, ml_g, c0, n0, m0)


def _lambda(lam_ref, lam_init):
    lp = lam_ref[...]
    a = jnp.sum(lp[0:1] * lp[1:2], axis=1, keepdims=True)
    b = jnp.sum(lp[2:3] * lp[3:4], axis=1, keepdims=True)
    return jnp.exp(a) - jnp.exp(b) + lam_init


def _split_q(q):
    lane = lax.broadcasted_iota(jnp.int32, q.shape, 1)
    zero = jnp.zeros_like(q)
    return jnp.where(lane < DA_DQK, q, zero), jnp.where(lane >= DA_DQK, q, zero)


def _attn_prompt_kernel(lam_ref, dag_ref, q_ref, k_ref, v_ref, o_ref, *, blk, nq, lam_init):
    lam = _lambda(lam_ref, lam_init)
    rows = lax.broadcasted_iota(jnp.int32, (2 * blk, blk), 0)
    cols = lax.broadcasted_iota(jnp.int32, (2 * blk, blk), 1)
    mask = (cols // ATT_CHUNK) <= ((rows % blk) // ATT_CHUNK)
    half = blk // 2
    st = [dict(s=[], p=[]) for _ in range(nq)]
    keys = lambda j: slice(j * blk, (j + 1) * blk)

    def lane_fold(x, op):
        return op(x[:, :half], x[:, half:])

    def pass1(c):
        b = st[c]

        def start():
            b["q2"] = jnp.concatenate(_split_q(q_ref[keys(c), :]), axis=0)

        def chunk(j):
            s = _dot_nt(b["q2"], k_ref[keys(j), :])
            if j == c:
                s = jnp.where(mask, s, -jnp.inf)
            mj = lane_fold(s, jnp.maximum)
            b["m"] = mj if j == 0 else jnp.maximum(b["m"], mj)
            b["s"].append(s)

        def finish():
            b["m"] = jnp.max(b["m"], axis=1, keepdims=True)

        return [start] + [functools.partial(chunk, j) for j in range(c + 1)] + [finish]

    def pass2(c):
        b = st[c]

        def chunk(j):
            p = jnp.exp2(b["s"][j] - b["m"])
            lj = lane_fold(p, jnp.add)
            b["l"] = lj if j == 0 else b["l"] + lj
            b["p"].append(p)

        def finish():
            l = jnp.sum(b["l"], axis=1, keepdims=True)
            b["l0"] = l[:blk]
            b["rho"] = lam * l[:blk] / l[blk:]

        return [functools.partial(chunk, j) for j in range(c + 1)] + [finish]

    def pass3(c):
        b = st[c]

        def chunk(j):
            p = b["p"][j]
            part = _dot((p[:blk] - b["rho"] * p[blk:]).astype(BF16), v_ref[keys(j), :])
            b["o"] = part if j == 0 else b["o"] + part

        def finish():
            o = b["o"] / b["l0"]
            o_ref[keys(c), :] = (_rms(o, dag_ref[...]) * (1.0 - lam_init)).astype(BF16)
            b.clear()

        return [functools.partial(chunk, j) for j in range(c + 1)] + [finish]

    for stage in range(nq + 2):
        active = [make(stage - lag) for lag, make in enumerate((pass1, pass2, pass3)) if 0 <= stage - lag < nq]
        for group in itertools.zip_longest(*active):
            for thunk in group:
                if thunk is not None:
                    thunk()


def _attn_prompt(z, da_lambda, da_norm, layer, bsz, seq, blk):
    nq = seq // blk
    lam_init = 0.8 - 0.6 * math.exp(-0.3 * layer)
    lay3 = lambda b, h: (layer, 0, 0)
    whole = lambda slab: pl.BlockSpec((seq, DA_DV), lambda b, h: (b, slab * DA_HEADS + h))
    return pl.pallas_call(
        functools.partial(_attn_prompt_kernel, blk=blk, nq=nq, lam_init=lam_init),
        grid=(bsz, DA_HEADS),
        in_specs=[
            pl.BlockSpec((None, 4, DA_DQK), lay3),
            pl.BlockSpec((None, 1, DA_DV), lay3),
            whole(Z_DQ), whole(Z_DK), whole(Z_DV),
        ],
        out_specs=pl.BlockSpec((seq, DA_DV), lambda b, h: (b, h)),
        out_shape=jax.ShapeDtypeStruct((bsz * seq, HEAD_W), BF16),
        compiler_params=_params(2),
        name="attn_prompt",
    )(da_lambda, da_norm, z, z, z)


def _attn_sample_kernel(lam_ref, dag_ref, q_ref, kn_ref, vn_ref, kp_ref, vp_ref, o_ref, *, past, lam_init):
    lam = _lambda(lam_ref, lam_init)
    t = q_ref.shape[0]
    for h in range(DA_HEADS):
        hs = slice(h * DA_DV, (h + 1) * DA_DV)
        q2 = jnp.concatenate(_split_q(q_ref[:, hs]), axis=0)
        kp = kp_ref[pl.ds(h, past, stride=DA_HEADS), :].astype(BF16)
        vp = vp_ref[pl.ds(h, past, stride=DA_HEADS), :].astype(BF16)
        sp = _dot_nt(q2, kp)
        sn = _dot_nt(q2, kn_ref[:, hs])
        m = jnp.maximum(jnp.max(sp, axis=1, keepdims=True), jnp.max(sn, axis=1, keepdims=True))
        pp = jnp.exp2(sp - m)
        pn = jnp.exp2(sn - m)
        l = jnp.sum(pp, axis=1, keepdims=True) + jnp.sum(pn, axis=1, keepdims=True)
        rho = lam * l[:t] / l[t:]
        o = (_dot((pp[:t] - rho * pp[t:]).astype(BF16), vp)
             + _dot((pn[:t] - rho * pn[t:]).astype(BF16), vn_ref[:, hs]))
        o = o / l[:t]
        o_ref[:, hs] = (_rms(o, dag_ref[...]) * (1.0 - lam_init)).astype(BF16)


def _attn_sample(z, cache_k, cache_v, da_lambda, da_norm, layer, bsz, seq):
    past = cache_k.shape[2] // DA_HEADS
    lam_init = 0.8 - 0.6 * math.exp(-0.3 * layer)
    lay3 = lambda b: (layer, 0, 0)
    cache = pl.BlockSpec((None, None, past * DA_HEADS, DA_DV), lambda b: (layer, b, 0, 0))
    slab = lambda col: pl.BlockSpec((seq, HEAD_W), lambda b: (b, col))
    return pl.pallas_call(
        functools.partial(_attn_sample_kernel, past=past, lam_init=lam_init),
        grid=(bsz,),
        in_specs=[
            pl.BlockSpec((None, 4, DA_DQK), lay3),
            pl.BlockSpec((None, 1, DA_DV), lay3),
            slab(Z_DQ), slab(Z_DK), slab(Z_DV),
            cache, cache,
        ],
        out_specs=pl.BlockSpec((seq, HEAD_W), lambda b: (b, 0)),
        out_shape=jax.ShapeDtypeStruct((bsz * seq, HEAD_W), BF16),
        compiler_params=_params(1),
        name="attn_sample",
    )(da_lambda, da_norm, z, z, z, cache_k, cache_v)


def _causal_conv3(u, hist, w):
    row = lax.broadcasted_iota(jnp.int32, (8, u.shape[1]), 0)
    r1 = pltpu.roll(u, 1, 0)
    r2 = pltpu.roll(u, 2, 0)
    top1 = jnp.where(row == 0, hist[1:2], r1[:8])
    top2 = jnp.where(row == 0, hist[0:1], jnp.where(row == 1, hist[1:2], r2[:8]))
    x1 = jnp.concatenate([top1, r1[8:]], axis=0)
    x2 = jnp.concatenate([top2, r2[8:]], axis=0)
    return w[2:3] * u + w[1:2] * x1 + w[0:1] * x2


def _init_history(hbuf, hist_ref):
    @pl.when(pl.program_id(1) == 0)
    def _():
        hbuf[...] = hist_ref[0]


def _merge_kernel(x_ref, sb_ref, sc_ref, sx_ref, g0_ref, g1_ref, g2_ref, ya_ref, yc_ref, hist_ref,
                  scw_ref, wbr_ref, wo_ref, x1_ref, cs_ref, hbuf, *, tm):
    _init_history(hbuf, hist_ref)
    u = sc_ref[...].astype(F32) * sx_ref[...].astype(F32)
    cu = _causal_conv3(u, hbuf[...], scw_ref[...])
    hbuf[...] = u[tm - HIST:tm, :]
    cs_ref[0] = u[tm - HIST:tm, :]
    yb = (sb_ref[...].astype(F32) * cu).astype(BF16)
    mix = (_sigmoid(g0_ref[...].astype(F32)) * _dot(ya_ref[...], wbr_ref[0])
           + _sigmoid(g1_ref[...].astype(F32)) * _dot(yb, wbr_ref[1])
           + _sigmoid(g2_ref[...].astype(F32)) * _dot(yc_ref[...], wbr_ref[2]))
    x1_ref[...] = x_ref[...] + _dot(mix.astype(BF16), wo_ref[...])


def _merge(x2d, z, ya, yc, hist, sc_w, w_br, w_o, layer, bsz, seq, tm):
    n, d = x2d.shape
    nt = seq // tm
    row = lambda col: (lambda b, j: (b * nt + j, col))
    lay3 = lambda b, j: (layer, 0, 0)
    state = lambda b, j: (b, 0, 0)
    return pl.pallas_call(
        functools.partial(_merge_kernel, tm=tm),
        grid=(bsz, nt),
        in_specs=[
            pl.BlockSpec((tm, d), row(0)),
            pl.BlockSpec((tm, HEAD_W), row(Z_SB)),
            pl.BlockSpec((tm, HEAD_W), row(Z_SC)),
            pl.BlockSpec((tm, HEAD_W), row(Z_SX)),
            pl.BlockSpec((tm, d), row(Z_GT)),
            pl.BlockSpec((tm, d), row(Z_GT + 1)),
            pl.BlockSpec((tm, d), row(Z_GT + 2)),
            pl.BlockSpec((tm, HEAD_W), row(0)),
            pl.BlockSpec((tm, HEAD_W), row(0)),
            pl.BlockSpec((1, HIST, SC_W), state),
            pl.BlockSpec((None, CONV_W, SC_W), lay3),
            _resident((None, 3, HEAD_W, d), lambda b, j: (layer, 0, 0, 0)),
            _resident((None, d, d), lay3),
        ],
        out_specs=[
            pl.BlockSpec((tm, d), row(0)),
            pl.BlockSpec((1, HIST, SC_W), state),
        ],
        out_shape=[
            jax.ShapeDtypeStruct((n, d), F32),
            jax.ShapeDtypeStruct((bsz, HIST, SC_W), F32),
        ],
        scratch_shapes=[pltpu.VMEM((HIST, SC_W), F32)],
        compiler_params=_params(2),
        name="merge",
    )(x2d, z, z, z, z, z, z, ya, yc, hist, sc_w, w_br, w_o)


def _ffn_kernel(x_ref, g2_ref, gf_ref, hist_ref, fw_ref, wup_ref, wdn_ref, y_ref, fs_ref, hbuf,
                *, tm, d_ff, chunk, group, final_norm):
    _init_history(hbuf, hist_ref)
    x = x_ref[...]
    h2 = _rms(x, g2_ref[...]).astype(BF16)
    n_chunks = d_ff // chunk

    def up_proj(i):
        lo = slice(i * chunk, (i + 1) * chunk)
        hi = slice(d_ff + i * chunk, d_ff + (i + 1) * chunk)
        return (lo, _dot(h2, wup_ref[:, lo])), (hi, _dot(h2, wup_ref[:, hi]))

    def conv(cols, up):
        out = _causal_conv3(up, hbuf[:, cols], fw_ref[:, cols])
        hbuf[:, cols] = up[tm - HIST:tm, :]
        return out

    ahead = up_proj(0)
    acc = None
    pieces = []
    for i in range(n_chunks):
        (lo, ua), (hi, ub) = ahead
        if i + 1 < n_chunks:
            ahead = up_proj(i + 1)
        a = conv(lo, ua)
        b = conv(hi, ub)
        pieces.append((a * _sigmoid(a) * b).astype(BF16))
        if len(pieces) == group or i + 1 == n_chunks:
            k0 = (i + 1 - len(pieces)) * chunk
            part = _dot(jnp.concatenate(pieces, axis=1), wdn_ref[k0:(i + 1) * chunk, :])
            acc = part if acc is None else acc + part
            pieces = []
    fs_ref[0] = hbuf[...]
    y = x + acc
    if final_norm:
        y = _rms(y, gf_ref[...])
    y_ref[...] = y


def _ffn(x2d, norm2, norm_f, hist, w_fconv, w_up, w_down, layer, bsz, seq, tm, final_norm):
    n, d = x2d.shape
    d_ff = w_down.shape[1]
    nt = seq // tm
    chunk = 256 if d_ff % 256 == 0 else LANES
    row = lambda b, j: (b * nt + j, 0)
    lay3 = lambda b, j: (layer, 0, 0)
    state = lambda b, j: (b, 0, 0)
    return pl.pallas_call(
        functools.partial(_ffn_kernel, tm=tm, d_ff=d_ff, chunk=chunk, group=FFN_DOWN_GROUP,
                          final_norm=final_norm),
        grid=(bsz, nt),
        in_specs=[
            pl.BlockSpec((tm, d), row),
            pl.BlockSpec((None, 1, d), lay3),
            pl.BlockSpec((1, d), lambda b, j: (0, 0)),
            pl.BlockSpec((1, HIST, 2 * d_ff), state),
            pl.BlockSpec((None, CONV_W, 2 * d_ff), lay3),
            _resident((None, d, 2 * d_ff), lay3),
            _resident((None, d_ff, d), lay3),
        ],
        out_specs=[
            pl.BlockSpec((tm, d), row),
            pl.BlockSpec((1, HIST, 2 * d_ff), state),
        ],
        out_shape=[
            jax.ShapeDtypeStruct((n, d), F32),
            jax.ShapeDtypeStruct((bsz, HIST, 2 * d_ff), F32),
        ],
        scratch_shapes=[pltpu.VMEM((HIST, 2 * d_ff), F32)],
        compiler_params=_params(2),
        name="ffn",
    )(x2d, norm2, norm_f, hist, w_fconv, w_up, w_down)


def _tile(seq, target):
    return target if seq % target == 0 else seq


def _group_layer(x2d, layer, depth, bsz, seq, wts, past, kv_bufs, final_norm):
    length = _tile(seq, 256)
    tm = _tile(bsz * seq, 512)
    if past is None:
        c0 = jnp.zeros((bsz, ML_HEADS, ML_D, ML_D), F32)
        n0 = jnp.zeros((bsz, ML_HEADS, ML_D), F32)
        m0 = jnp.zeros((bsz, ML_HEADS, LANES), F32)
        conv0 = jnp.zeros((bsz, HIST, SC_W), F32)
        ffn0 = jnp.zeros((bsz, HIST, wts["w_fconv"].shape[2]), F32)
    else:
        c0, n0 = past["C"][layer], past["n"][layer]
        m0 = jnp.broadcast_to(past["m"][layer][..., None], (bsz, ML_HEADS, LANES))
        conv0, ffn0 = past["conv"][layer], past["ffn"][layer]
    proj_args = (x2d, wts["norm1"], wts["w_head"], wts["w_tail"], wts["w_gate"], wts["w_gate_t"],
                 kv_bufs, layer, depth, tm)
    scan_args = (wts["bias_row"], wts["bias_col"], wts["ml_norm"], c0, n0, m0)
    if seq % tm == 0 and tm % length == 0:
        z, k_buf, v_buf, ya, c_new, n_new, m_new = _in_proj(*proj_args, scan=scan_args + (seq, length))
    else:
        z, k_buf, v_buf, gate, gate_t = _in_proj(*proj_args)
        gate_t = gate_t.reshape(GATE_W, bsz * seq // length, length).swapaxes(0, 1)
        ya, c_new, n_new, m_new = _mlstm(z, gate, gate_t, *scan_args, layer, bsz, seq, length)
    if past is None:
        yc = _attn_prompt(z, wts["da_lambda"], wts["da_norm"], layer, bsz, seq, 256)
    else:
        yc = _attn_sample(z, past["k"], past["v"], wts["da_lambda"], wts["da_norm"], layer, bsz, seq)
    x1, conv_new = _merge(x2d, z, ya, yc, conv0, wts["w_sc_conv"], wts["w_branch"], wts["w_out"],
                          layer, bsz, seq, _tile(seq, 512))
    x2, ffn_new = _ffn(x1, wts["norm2"], wts["norm_f"], ffn0, wts["w_fconv"], wts["w_up"], wts["w_down"],
                       layer, bsz, seq, _tile(seq, 512), final_norm)
    return x2, (k_buf, v_buf), (c_new, n_new, m_new[:, :, 0], conv_new, ffn_new)


def kernel(x_prompt, x_sample, cache_k, cache_v, state_C, state_n, state_m, state_conv, state_ffn,
           norm1, w_in, b_if, ml_norm, w_sc_conv, da_lambda, da_norm, w_branch, w_out,
           norm2, w_up, w_fconv, w_down, norm_f):
    depth, d, in_w = w_in.shape
    assert in_w == Z_W + GATE_W and d == 1024
    bp, sp, _ = x_prompt.shape
    bs, ss, _ = x_sample.shape
    past_len = cache_k.shape[2]
    g0 = 4 * HEAD_W

    w_gate = jnp.pad(w_in[:, :, g0:g0 + GATE_W], ((0, 0), (0, 0), (0, LANES - GATE_W))).astype(BF16)
    bias = jnp.concatenate([b_if[:, 0], b_if[:, 1]], axis=-1).astype(F32)
    wts = {
        "norm1": norm1.reshape(depth, 1, d),
        "w_head": w_in[:, :, :g0].astype(BF16),
        "w_tail": w_in[:, :, g0 + GATE_W:].astype(BF16),
        "w_gate": w_gate,
        "w_gate_t": jnp.swapaxes(w_gate, 1, 2),
        "bias_row": jnp.pad(bias, ((0, 0), (0, LANES - GATE_W))).reshape(depth, 1, LANES),
        "bias_col": bias.reshape(depth, GATE_W, 1),
        "ml_norm": ml_norm,
        "w_sc_conv": w_sc_conv,
        "da_lambda": da_lambda,
        "da_norm": da_norm.reshape(depth, 1, DA_DV),
        "w_branch": w_branch.astype(BF16),
        "w_out": w_out.astype(BF16),
        "norm2": norm2.reshape(depth, 1, d),
        "w_up": w_up.astype(BF16),
        "w_fconv": w_fconv,
        "w_down": w_down.astype(BF16),
        "norm_f": norm_f.reshape(1, d),
    }
    past = {
        "k": cache_k.reshape(depth, bs, past_len * DA_HEADS, 2 * DA_DQK),
        "v": cache_v.reshape(depth, bs, past_len * DA_HEADS, DA_DV),
        "C": state_C, "n": state_n, "m": state_m, "conv": state_conv, "ffn": state_ffn,
    }

    xp = x_prompt.reshape(bp * sp, d)
    xs = x_sample.reshape(bs * ss, d)
    p_states, s_states = [], []
    p_kv = s_kv = None
    for layer in range(depth):
        last = layer == depth - 1
        xp, p_kv, st = _group_layer(xp, layer, depth, bp, sp, wts, None, p_kv, last)
        p_states.append(st)
        xs, s_kv, st = _group_layer(xs, layer, depth, bs, ss, wts, past, s_kv, last)
        s_states.append(st)
    kv_shape = lambda b, s: (depth, b, s, DA_HEADS, DA_DV)
    p_out = [t.reshape(kv_shape(bp, sp)) for t in p_kv] + [jnp.stack(t) for t in zip(*p_states)]
    s_out = [t.reshape(kv_shape(bs, ss)) for t in s_kv] + [jnp.stack(t) for t in zip(*s_states)]
    return (xp.reshape(bp, sp, d), xs.reshape(bs, ss, d), *p_out, *s_out)
```

```python
import functools
import itertools
import math

import jax
import jax.numpy as jnp
from jax import lax
from jax.experimental import pallas as pl
from jax.experimental.pallas import tpu as pltpu

F32 = jnp.float32
BF16 = jnp.bfloat16

EPS = 1e-6
ML_HEADS = 4
ML_D = 128
SC_W = 512
DA_HEADS = 4
DA_DQK = 64
DA_DV = 128
ATT_CHUNK = 64
CONV_W = 3
HIST = CONV_W - 1
FFN_DOWN_GROUP = 4
GATE_W = 2 * ML_HEADS
GATE_T_ROWS = 16
GROUP_ROWS = 256
LANES = 128
HEAD_W = ML_HEADS * ML_D
Z_W = 10 * HEAD_W + 3 * 1024
LN_K_SCALE = -0.5 * math.log(ML_D)
Q_SCALE = DA_DQK ** -0.5 * math.log2(math.e)
VMEM_LIMIT = 56 * 2 ** 20

Z_MQ, Z_MK, Z_MV, Z_MO, Z_SB, Z_SC, Z_SX, Z_DQ, Z_DK, Z_DV = range(10)
Z_GT = 5


def _params(n_axes):
    return pltpu.CompilerParams(dimension_semantics=("arbitrary",) * n_axes,
                                vmem_limit_bytes=VMEM_LIMIT)


def _resident(block_shape, index_map):
    return pl.BlockSpec(block_shape, index_map, pipeline_mode=pl.Buffered(1))


def _rms(x, g):
    ms = jnp.mean(x * x, axis=-1, keepdims=True)
    return x * lax.rsqrt(ms + EPS) * g


def _sigmoid(x):
    return 0.5 * jnp.tanh(0.5 * x) + 0.5


def _dot(a, b):
    return jnp.dot(a, b, preferred_element_type=F32)


def _dot_nt(a, b):
    return lax.dot_general(a, b, (((1,), (1,)), ((), ())), preferred_element_type=F32)


def _dot_tn(a, b):
    return lax.dot_general(a, b, (((0,), (0,)), ((), ())), preferred_element_type=F32)


def _inproj_kernel(x_ref, g_ref, wa_ref, wb_ref, wg_ref, wgt_ref, *rest, tm, first, scan):
    rest = list(rest)
    if scan:
        brow_ref, bcol_ref, mlg_ref, c0_ref, n0_ref, m0_ref = rest[:6]
        rest = rest[6:]
    if not first:
        rest = rest[2:]
    if scan:
        z_ref, k_ref, v_ref, ya_ref, c_ref, n_ref, m_ref = rest
    else:
        z_ref, k_ref, v_ref, gate_ref, gatet_ref = rest
    h = _rms(x_ref[...], g_ref[...]).astype(BF16)

    def project(c):
        cols = slice(c * HEAD_W, (c + 1) * HEAD_W)
        w = wa_ref[:, cols] if c < Z_SB else wb_ref[:, (c - Z_SB) * HEAD_W:(c - Z_SB + 1) * HEAD_W]
        zc = _dot(h, w)
        if c == Z_DQ:
            zc = zc * Q_SCALE
        z_ref[:, cols] = zc.astype(BF16)
        if c == Z_DK:
            _store_heads(k_ref, zc, tm, first)
        if c == Z_DV:
            _store_heads(v_ref, zc, tm, first)

    scan_slabs = (Z_MQ, Z_MK, Z_MV, Z_MO)
    for c in scan_slabs:
        project(c)
    gate = _dot(h, wg_ref[...])
    gate_t = _dot_nt(wgt_ref[...], h)[:GATE_W, :]
    others = [c for c in range(Z_W // HEAD_W) if c not in scan_slabs]
    if not scan:
        for c in others:
            project(c)
        gate_ref[...] = gate
        gatet_ref[...] = gate_t
        return

    length, tiles_per_seq = scan

    @pl.when(lax.rem(pl.program_id(0), tiles_per_seq) == 0)
    def _():
        c_ref[...] = c0_ref[...]
        n_ref[...] = n0_ref[...]
        m_ref[...] = m0_ref[...]

    def chunk_stages(start):
        rows = slice(start, start + length)

        def load(slab, hd):
            return z_ref[rows, slab * HEAD_W + hd * ML_D:slab * HEAD_W + (hd + 1) * ML_D]

        def store(hd, val):
            ya_ref[rows, hd * ML_D:(hd + 1) * ML_D] = val

        return _mlstm_stages(load, store, gate[rows, :], gate_t[:, rows], brow_ref, bcol_ref, mlg_ref,
                             c_ref, n_ref, m_ref, length)

    stages = itertools.chain(*[chunk_stages(start) for start in range(0, tm, length)])
    per_chunk = -(-MLSTM_STAGES * (tm // length) // len(others))
    for c in others:
        project(c)
        for _ in itertools.islice(stages, per_chunk):
            pass
    for _ in stages:
        pass


def _store_heads(ref, slab, tm, first):
    dst = ref.at[0] if first else ref
    for h in range(DA_HEADS):
        dst[pl.ds(h, tm, stride=DA_HEADS), :] = slab[:, h * DA_DV:(h + 1) * DA_DV]
    if first:
        ref[1:] = jnp.zeros((ref.shape[0] - 1,) + ref.shape[1:], F32)


def _in_proj(x2d, norm1, w_head, w_tail, w_gate, w_gate_t, kv_bufs, layer, depth, tm, scan=None):
    n, d = x2d.shape
    first = kv_bufs is None
    row = lambda i: (i, 0)
    lay3 = lambda i: (layer, 0, 0)
    scan_specs, scan_args, scan_cfg = [], (), None
    tail_specs = [pl.BlockSpec((tm, LANES), row), pl.BlockSpec((GATE_W, tm), lambda i: (0, i))]
    tail_shapes = [jax.ShapeDtypeStruct((n, LANES), F32), jax.ShapeDtypeStruct((GATE_W, n), F32)]
    if scan is not None:
        *scan_args, seq, length = scan
        tiles = seq // tm
        bsz = n // seq
        scan_cfg = (length, tiles)
        state4 = lambda i: (i // tiles, 0, 0, 0)
        state3 = lambda i: (i // tiles, 0, 0)
        state_specs = [pl.BlockSpec((1, ML_HEADS, ML_D, ML_D), state4),
                       pl.BlockSpec((1, ML_HEADS, ML_D), state3),
                       pl.BlockSpec((1, ML_HEADS, LANES), state3)]
        scan_specs = [pl.BlockSpec((None, 1, LANES), lay3), pl.BlockSpec((None, GATE_W, 1), lay3),
                      pl.BlockSpec((None, ML_HEADS, ML_D), lay3)] + state_specs
        tail_specs = [pl.BlockSpec((tm, HEAD_W), row)] + state_specs
        tail_shapes = [jax.ShapeDtypeStruct((n, HEAD_W), BF16),
                       jax.ShapeDtypeStruct((bsz, ML_HEADS, ML_D, ML_D), F32),
                       jax.ShapeDtypeStruct((bsz, ML_HEADS, ML_D), F32),
                       jax.ShapeDtypeStruct((bsz, ML_HEADS, LANES), F32)]
    if first:
        kv_spec = pl.BlockSpec((depth, tm * DA_HEADS, DA_DV), lambda i: (0, i, 0))
        kv_specs, kv_args, aliases = [], (), {}
    else:
        kv_spec = pl.BlockSpec((None, tm * DA_HEADS, DA_DV), lambda i: (layer, i, 0))
        kv_specs = [pl.BlockSpec(memory_space=pl.ANY)] * 2
        kv_args = tuple(kv_bufs)
        first_kv = 6 + len(scan_specs)
        aliases = {first_kv: 1, first_kv + 1: 2}
    kv_shape = jax.ShapeDtypeStruct((depth, n * DA_HEADS, DA_DV), F32)
    return pl.pallas_call(
        functools.partial(_inproj_kernel, tm=tm, first=first, scan=scan_cfg),
        grid=(n // tm,),
        in_specs=[
            pl.BlockSpec((tm, d), row),
            pl.BlockSpec((None, 1, d), lay3),
            _resident((None, d, Z_SB * HEAD_W), lay3),
            _resident((None, d, Z_W - Z_SB * HEAD_W), lay3),
            _resident((None, d, LANES), lay3),
            _resident((None, GATE_T_ROWS, d), lay3),
        ] + scan_specs + kv_specs,
        out_specs=[pl.BlockSpec((tm, Z_W), row), kv_spec, kv_spec] + tail_specs,
        out_shape=[jax.ShapeDtypeStruct((n, Z_W), BF16), kv_shape, kv_shape] + tail_shapes,
        input_output_aliases=aliases,
        compiler_params=_params(1),
        name="in_proj_scan" if scan is not None else "in_proj",
    )(x2d, norm1, w_head, w_tail, w_gate, w_gate_t, *scan_args, *kv_args)


def _log_sigmoid(x):
    return jnp.minimum(x, 0.0) - jnp.log1p(jnp.exp(-jnp.abs(x)))


def _split_bf16(x):
    hi = x.astype(BF16)
    lo = (x - hi.astype(F32)).astype(BF16)
    return hi, lo


MLSTM_STAGES = 16


def _mlstm_stages(load, store, gate, gate_t, brow_ref, bcol_ref, mlg_ref, c_ref, n_ref, m_ref, L):
    rows = lax.broadcasted_iota(jnp.int32, (L, L), 0)
    cols = lax.broadcasted_iota(jnp.int32, (L, L), 1)
    tri = jnp.where(rows >= cols, 1.0, 0.0).astype(BF16)
    visible = rows <= cols

    pre_c = gate + brow_ref[...]
    pre_r = gate_t + bcol_ref[...]
    hi, lo = _split_bf16(_log_sigmoid(pre_c))
    cum_c = _dot(tri, hi) + _dot(tri, lo)
    hi, lo = _split_bf16(_log_sigmoid(pre_r))
    cum_r = _dot_nt(hi, tri) + _dot_nt(lo, tri)
    yield

    heads = range(ML_HEADS)
    q = [load(Z_MQ, h) for h in heads]
    k = [load(Z_MK, h) for h in heads]
    v = [load(Z_MV, h) for h in heads]
    b_row = [cum_r[ML_HEADS + h:ML_HEADS + h + 1, :] for h in heads]
    c_key = [pre_c[:, h:h + 1] - cum_c[:, ML_HEADS + h:ML_HEADS + h + 1] for h in heads]
    m_prev = [m_ref[0, h:h + 1, 0:1] for h in heads]
    c_prev = [c_ref[0, h] for h in heads]
    n_prev = [n_ref[0, h:h + 1, :] for h in heads]
    yield

    d = [jnp.where(visible, c_key[h], -jnp.inf) for h in heads]
    yield
    mx = [jnp.maximum(jnp.max(d[h], axis=0, keepdims=True), m_prev[h]) for h in heads]
    yield
    w = [jnp.exp(d[h] + LN_K_SCALE - mx[h]) for h in heads]
    inter = [jnp.exp(m_prev[h] - mx[h]) for h in heads]
    yield
    s = [_dot_nt(k[h], q[h]) * w[h] for h in heads]
    yield
    carried = [lax.dot_general(c_prev[h].astype(BF16), q[h], (((0,), (1,)), ((), ())),
                               preferred_element_type=F32) for h in heads]
    yield
    num = [_dot_tn(v[h], s[h].astype(BF16)) + inter[h] * carried[h] for h in heads]
    yield
    qn = [_dot_nt(n_prev[h].astype(BF16), q[h]) for h in heads]
    den = [jnp.sum(s[h], axis=0, keepdims=True) + inter[h] * qn[h] for h in heads]
    yield
    den = [jnp.maximum(jnp.abs(den[h]), jnp.exp(-(b_row[h] + mx[h]))) for h in heads]
    hid = [num[h] / den[h] for h in heads]
    yield
    hid = [hid[h] * lax.rsqrt(jnp.mean(hid[h] * hid[h], axis=0, keepdims=True) + EPS) for h in heads]
    yield
    for h in heads:
        gain = _sigmoid(load(Z_MO, h).astype(F32)) * mlg_ref[h:h + 1, :]
        store(h, (hid[h].T * gain).astype(BF16))
    yield

    mx_last = [mx[h][:, L - 1:L] for h in heads]
    il = [jnp.exp(m_prev[h] - mx_last[h]) for h in heads]
    kw = [k[h].astype(F32) * jnp.exp(c_key[h] + LN_K_SCALE - mx_last[h]) for h in heads]
    yield
    kv = [_dot_tn(kw[h].astype(BF16), v[h]) for h in heads]
    yield
    for h in heads:
        c_ref[0, h] = il[h] * c_prev[h] + kv[h]
        n_ref[0, h:h + 1, :] = il[h] * n_prev[h] + jnp.sum(kw[h], axis=0, keepdims=True)
    yield
    for h in heads:
        m_ref[0, h:h + 1, :] = jnp.broadcast_to(b_row[h][:, L - 1:L] + mx_last[h], (1, LANES))
    yield


def _mlstm_kernel(q_ref, k_ref, v_ref, o_ref, gate_ref, gatet_ref, brow_ref, bcol_ref, mlg_ref,
                  c0_ref, n0_ref, m0_ref, ya_ref, c_ref, n_ref, m_ref, *, length):
    @pl.when(pl.program_id(1) == 0)
    def _():
        c_ref[...] = c0_ref[...]
        n_ref[...] = n0_ref[...]
        m_ref[...] = m0_ref[...]

    slabs = {Z_MQ: q_ref, Z_MK: k_ref, Z_MV: v_ref, Z_MO: o_ref}

    def load(slab, h):
        return slabs[slab][:, h * ML_D:(h + 1) * ML_D]

    def store(h, val):
        ya_ref[:, h * ML_D:(h + 1) * ML_D] = val

    for _ in _mlstm_stages(load, store, gate_ref[...], gatet_ref[...], brow_ref, bcol_ref, mlg_ref,
                           c_ref, n_ref, m_ref, length):
        pass


def _mlstm(z, gate, gate_t, bias_row, bias_col, ml_g, c0, n0, m0, layer, bsz, seq, length):
    n = bsz * seq
    nc = seq // length
    row = lambda col: (lambda b, j: (b * nc + j, col))
    lay3 = lambda b, j: (layer, 0, 0)
    state4 = lambda b, j: (b, 0, 0, 0)
    state3 = lambda b, j: (b, 0, 0)
    return pl.pallas_call(
        functools.partial(_mlstm_kernel, length=length),
        grid=(bsz, nc),
        in_specs=[
            pl.BlockSpec((length, HEAD_W), row(Z_MQ)),
            pl.BlockSpec((length, HEAD_W), row(Z_MK)),
            pl.BlockSpec((length, HEAD_W), row(Z_MV)),
            pl.BlockSpec((length, HEAD_W), row(Z_MO)),
            pl.BlockSpec((length, LANES), row(0)),
            pl.BlockSpec((None, GATE_W, length), lambda b, j: (b * nc + j, 0, 0)),
            pl.BlockSpec((None, 1, LANES), lay3),
            pl.BlockSpec((None, GATE_W, 1), lay3),
            pl.BlockSpec((None, ML_HEADS, ML_D), lay3),
            pl.BlockSpec((1, ML_HEADS, ML_D, ML_D), state4),
            pl.BlockSpec((1, ML_HEADS, ML_D), state3),
            pl.BlockSpec((1, ML_HEADS, LANES), state3),
        ],
        out_specs=[
            pl.BlockSpec((length, HEAD_W), row(0)),
            pl.BlockSpec((1, ML_HEADS, ML_D, ML_D), state4),
            pl.BlockSpec((1, ML_HEADS, ML_D), state3),
            pl.BlockSpec((1, ML_HEADS, LANES), state3),
        ],
        out_shape=[
            jax.ShapeDtypeStruct((n, HEAD_W), BF16),
            jax.ShapeDtypeStruct((bsz, ML_HEADS, ML_D, ML_D), F32),
            jax.ShapeDtypeStruct((bsz, ML_HEADS, ML_D), F32),
            jax.ShapeDtypeStruct((bsz, ML_HEADS, LANES), F32),
        ],
        compiler_params=_params(2),
        name="mlstm",
    )(z, z, z, z, gate, gate_t, bias_row, bias_col, ml_g, c0, n0, m0)


def _lambda(lam_ref, lam_init):
    lp = lam_ref[...]
    a = jnp.sum(lp[0:1] * lp[1:2], axis=1, keepdims=True)
    b = jnp.sum(lp[2:3] * lp[3:4], axis=1, keepdims=True)
    return jnp.exp(a) - jnp.exp(b) + lam_init


def _split_q(q):
    lane = lax.broadcasted_iota(jnp.int32, q.shape, 1)
    zero = jnp.zeros_like(q)
    return jnp.where(lane < DA_DQK, q, zero), jnp.where(lane >= DA_DQK, q, zero)


def _attn_prompt_kernel(lam_ref, dag_ref, q_ref, k_ref, v_ref, o_ref, *, blk, nq, lam_init):
    lam = _lambda(lam_ref, lam_init)
    rows = lax.broadcasted_iota(jnp.int32, (2 * blk, blk), 0)
    cols = lax.broadcasted_iota(jnp.int32, (2 * blk, blk), 1)
    mask = (cols // ATT_CHUNK) <= ((rows % blk) // ATT_CHUNK)
    half = blk // 2
    st = [dict(s=[], p=[]) for _ in range(nq)]
    keys = lambda j: slice(j * blk, (j + 1) * blk)

    def lane_fold(x, op):
        return op(x[:, :half], x[:, half:])

    def pass1(c):
        b = st[c]

        def start():
            b["q2"] = jnp.concatenate(_split_q(q_ref[keys(c), :]), axis=0)

        def chunk(j):
            s = _dot_nt(b["q2"], k_ref[keys(j), :])
            if j == c:
                s = jnp.where(mask, s, -jnp.inf)
            mj = lane_fold(s, jnp.maximum)
            b["m"] = mj if j == 0 else jnp.maximum(b["m"], mj)
            b["s"].append(s)

        def finish():
            b["m"] = jnp.max(b["m"], axis=1, keepdims=True)

        return [start] + [functools.partial(chunk, j) for j in range(c + 1)] + [finish]

    def pass2(c):
        b = st[c]

        def chunk(j):
            p = jnp.exp2(b["s"][j] - b["m"])
            lj = lane_fold(p, jnp.add)
            b["l"] = lj if j == 0 else b["l"] + lj
            b["p"].append(p)

        def finish():
            l = jnp.sum(b["l"], axis=1, keepdims=True)
            b["l0"] = l[:blk]
            b["rho"] = lam * l[:blk] / l[blk:]

        return [functools.partial(chunk, j) for j in range(c + 1)] + [finish]

    def pass3(c):
        b = st[c]

        def chunk(j):
            p = b["p"][j]
            part = _dot((p[:blk] - b["rho"] * p[blk:]).astype(BF16), v_ref[keys(j), :])
            b["o"] = part if j == 0 else b["o"] + part

        def finish():
            o = b["o"] / b["l0"]
            o_ref[keys(c), :] = (_rms(o, dag_ref[...]) * (1.0 - lam_init)).astype(BF16)
            b.clear()

        return [functools.partial(chunk, j) for j in range(c + 1)] + [finish]

    for stage in range(nq + 2):
        active = [make(stage - lag) for lag, make in enumerate((pass1, pass2, pass3)) if 0 <= stage - lag < nq]
        for group in itertools.zip_longest(*active):
            for thunk in group:
                if thunk is not None:
                    thunk()


def _attn_prompt(z, da_lambda, da_norm, layer, bsz, seq, blk):
    nq = seq // blk
    lam_init = 0.8 - 0.6 * math.exp(-0.3 * layer)
    lay3 = lambda b, h: (layer, 0, 0)
    whole = lambda slab: pl.BlockSpec((seq, DA_DV), lambda b, h: (b, slab * DA_HEADS + h))
    return pl.pallas_call(
        functools.partial(_attn_prompt_kernel, blk=blk, nq=nq, lam_init=lam_init),
        grid=(bsz, DA_HEADS),
        in_specs=[
            pl.BlockSpec((None, 4, DA_DQK), lay3),
            pl.BlockSpec((None, 1, DA_DV), lay3),
            whole(Z_DQ), whole(Z_DK), whole(Z_DV),
        ],
        out_specs=pl.BlockSpec((seq, DA_DV), lambda b, h: (b, h)),
        out_shape=jax.ShapeDtypeStruct((bsz * seq, HEAD_W), BF16),
        compiler_params=_params(2),
        name="attn_prompt",
    )(da_lambda, da_norm, z, z, z)


def _attn_sample_kernel(lam_ref, dag_ref, q_ref, kn_ref, vn_ref, kp_ref, vp_ref, o_ref, *, past, lam_init):
    lam = _lambda(lam_ref, lam_init)
    t = q_ref.shape[0]
    for h in range(DA_HEADS):
        hs = slice(h * DA_DV, (h + 1) * DA_DV)
        q2 = jnp.concatenate(_split_q(q_ref[:, hs]), axis=0)
        kp = kp_ref[pl.ds(h, past, stride=DA_HEADS), :].astype(BF16)
        vp = vp_ref[pl.ds(h, past, stride=DA_HEADS), :].astype(BF16)
        sp = _dot_nt(q2, kp)
        sn = _dot_nt(q2, kn_ref[:, hs])
        m = jnp.maximum(jnp.max(sp, axis=1, keepdims=True), jnp.max(sn, axis=1, keepdims=True))
        pp = jnp.exp2(sp - m)
        pn = jnp.exp2(sn - m)
        l = jnp.sum(pp, axis=1, keepdims=True) + jnp.sum(pn, axis=1, keepdims=True)
        rho = lam * l[:t] / l[t:]
        o = (_dot((pp[:t] - rho * pp[t:]).astype(BF16), vp)
             + _dot((pn[:t] - rho * pn[t:]).astype(BF16), vn_ref[:, hs]))
        o = o / l[:t]
        o_ref[:, hs] = (_rms(o, dag_ref[...]) * (1.0 - lam_init)).astype(BF16)


def _attn_sample(z, cache_k, cache_v, da_lambda, da_norm, layer, bsz, seq):
    past = cache_k.shape[2] // DA_HEADS
    lam_init = 0.8 - 0.6 * math.exp(-0.3 * layer)
    lay3 = lambda b: (layer, 0, 0)
    cache = pl.BlockSpec((None, None, past * DA_HEADS, DA_DV), lambda b: (layer, b, 0, 0))
    slab = lambda col: pl.BlockSpec((seq, HEAD_W), lambda b: (b, col))
    return pl.pallas_call(
        functools.partial(_attn_sample_kernel, past=past, lam_init=lam_init),
        grid=(bsz,),
        in_specs=[
            pl.BlockSpec((None, 4, DA_DQK), lay3),
            pl.BlockSpec((None, 1, DA_DV), lay3),
            slab(Z_DQ), slab(Z_DK), slab(Z_DV),
            cache, cache,
        ],
        out_specs=pl.BlockSpec((seq, HEAD_W), lambda b: (b, 0)),
        out_shape=jax.ShapeDtypeStruct((bsz * seq, HEAD_W), BF16),
        compiler_params=_params(1),
        name="attn_sample",
    )(da_lambda, da_norm, z, z, z, cache_k, cache_v)


def _causal_conv3(u, hist, w):
    row = lax.broadcasted_iota(jnp.int32, (8, u.shape[1]), 0)
    r1 = pltpu.roll(u, 1, 0)
    r2 = pltpu.roll(u, 2, 0)
    top1 = jnp.where(row == 0, hist[1:2], r1[:8])
    top2 = jnp.where(row == 0, hist[0:1], jnp.where(row == 1, hist[1:2], r2[:8]))
    x1 = jnp.concatenate([top1, r1[8:]], axis=0)
    x2 = jnp.concatenate([top2, r2[8:]], axis=0)
    return w[2:3] * u + w[1:2] * x1 + w[0:1] * x2


def _init_history(hbuf, hist_ref):
    @pl.when(pl.program_id(1) == 0)
    def _():
        hbuf[...] = hist_ref[...]


def _conv_sequences(u, hbuf, cols, w):
    group = hbuf.shape[0]
    rows_per = u.shape[0] // group
    out = []
    for g in range(group):
        ug = u[g * rows_per:(g + 1) * rows_per, :]
        out.append(_causal_conv3(ug, hbuf[g, :, cols], w))
        hbuf[g, :, cols] = ug[rows_per - HIST:rows_per, :]
    return out[0] if group == 1 else jnp.concatenate(out, axis=0)


def _merge_kernel(x_ref, sb_ref, sc_ref, sx_ref, g0_ref, g1_ref, g2_ref, ya_ref, yc_ref, hist_ref,
                  scw_ref, wbr_ref, wo_ref, x1_ref, cs_ref, hbuf, *, tm):
    _init_history(hbuf, hist_ref)
    u = sc_ref[...].astype(F32) * sx_ref[...].astype(F32)
    cu = _conv_sequences(u, hbuf, slice(None), scw_ref[...])
    cs_ref[...] = hbuf[...]
    yb = (sb_ref[...].astype(F32) * cu).astype(BF16)
    mix = (_sigmoid(g0_ref[...].astype(F32)) * _dot(ya_ref[...], wbr_ref[0])
           + _sigmoid(g1_ref[...].astype(F32)) * _dot(yb, wbr_ref[1])
           + _sigmoid(g2_ref[...].astype(F32)) * _dot(yc_ref[...], wbr_ref[2]))
    x1_ref[...] = x_ref[...] + _dot(mix.astype(BF16), wo_ref[...])


def _merge(x2d, z, ya, yc, hist, sc_w, w_br, w_o, layer, bsz, seq, tm):
    n, d = x2d.shape
    group, tm, nt = _tiling(bsz, seq, tm)
    row = lambda col: (lambda b, j: (b * nt + j, col))
    lay3 = lambda b, j: (layer, 0, 0)
    state = lambda b, j: (b, 0, 0)
    return pl.pallas_call(
        functools.partial(_merge_kernel, tm=tm),
        grid=(bsz // group, nt),
        in_specs=[
            pl.BlockSpec((tm, d), row(0)),
            pl.BlockSpec((tm, HEAD_W), row(Z_SB)),
            pl.BlockSpec((tm, HEAD_W), row(Z_SC)),
            pl.BlockSpec((tm, HEAD_W), row(Z_SX)),
            pl.BlockSpec((tm, d), row(Z_GT)),
            pl.BlockSpec((tm, d), row(Z_GT + 1)),
            pl.BlockSpec((tm, d), row(Z_GT + 2)),
            pl.BlockSpec((tm, HEAD_W), row(0)),
            pl.BlockSpec((tm, HEAD_W), row(0)),
            pl.BlockSpec((group, HIST, SC_W), state),
            pl.BlockSpec((None, CONV_W, SC_W), lay3),
            _resident((None, 3, HEAD_W, d), lambda b, j: (layer, 0, 0, 0)),
            _resident((None, d, d), lay3),
        ],
        out_specs=[
            pl.BlockSpec((tm, d), row(0)),
            pl.BlockSpec((group, HIST, SC_W), state),
        ],
        out_shape=[
            jax.ShapeDtypeStruct((n, d), F32),
            jax.ShapeDtypeStruct((bsz, HIST, SC_W), F32),
        ],
        scratch_shapes=[pltpu.VMEM((group, HIST, SC_W), F32)],
        compiler_params=_params(2),
        name="merge",
    )(x2d, z, z, z, z, z, z, ya, yc, hist, sc_w, w_br, w_o)


def _ffn_kernel(x_ref, g2_ref, gf_ref, hist_ref, fw_ref, wup_ref, wdn_ref, y_ref, fs_ref, hbuf,
                *, tm, d_ff, chunk, group, final_norm):
    _init_history(hbuf, hist_ref)
    x = x_ref[...]
    h2 = _rms(x, g2_ref[...]).astype(BF16)
    n_chunks = d_ff // chunk

    def up_proj(i):
        lo = slice(i * chunk, (i + 1) * chunk)
        hi = slice(d_ff + i * chunk, d_ff + (i + 1) * chunk)
        return (lo, _dot(h2, wup_ref[:, lo])), (hi, _dot(h2, wup_ref[:, hi]))

    def conv(cols, up):
        return _conv_sequences(up, hbuf, cols, fw_ref[:, cols])

    ahead = up_proj(0)
    acc = None
    pieces = []
    for i in range(n_chunks):
        (lo, ua), (hi, ub) = ahead
        if i + 1 < n_chunks:
            ahead = up_proj(i + 1)
        a = conv(lo, ua)
        b = conv(hi, ub)
        pieces.append((a * _sigmoid(a) * b).astype(BF16))
        if len(pieces) == group or i + 1 == n_chunks:
            k0 = (i + 1 - len(pieces)) * chunk
            part = _dot(jnp.concatenate(pieces, axis=1), wdn_ref[k0:(i + 1) * chunk, :])
            acc = part if acc is None else acc + part
            pieces = []
    fs_ref[...] = hbuf[...]
    y = x + acc
    if final_norm:
        y = _rms(y, gf_ref[...])
    y_ref[...] = y


def _ffn(x2d, norm2, norm_f, hist, w_fconv, w_up, w_down, layer, bsz, seq, tm, final_norm):
    n, d = x2d.shape
    d_ff = w_down.shape[1]
    group, tm, nt = _tiling(bsz, seq, tm)
    chunk = 256 if d_ff % 256 == 0 else LANES
    row = lambda b, j: (b * nt + j, 0)
    lay3 = lambda b, j: (layer, 0, 0)
    state = lambda b, j: (b, 0, 0)
    return pl.pallas_call(
        functools.partial(_ffn_kernel, tm=tm, d_ff=d_ff, chunk=chunk, group=FFN_DOWN_GROUP,
                          final_norm=final_norm),
        grid=(bsz // group, nt),
        in_specs=[
            pl.BlockSpec((tm, d), row),
            pl.BlockSpec((None, 1, d), lay3),
            pl.BlockSpec((1, d), lambda b, j: (0, 0)),
            pl.BlockSpec((group, HIST, 2 * d_ff), state),
            pl.BlockSpec((None, CONV_W, 2 * d_ff), lay3),
            _resident((None, d, 2 * d_ff), lay3),
            _resident((None, d_ff, d), lay3),
        ],
        out_specs=[
            pl.BlockSpec((tm, d), row),
            pl.BlockSpec((group, HIST, 2 * d_ff), state),
        ],
        out_shape=[
            jax.ShapeDtypeStruct((n, d), F32),
            jax.ShapeDtypeStruct((bsz, HIST, 2 * d_ff), F32),
        ],
        scratch_shapes=[pltpu.VMEM((group, HIST, 2 * d_ff), F32)],
        compiler_params=_params(2),
        name="ffn",
    )(x2d, norm2, norm_f, hist, w_fconv, w_up, w_down)


def _tiling(bsz, seq, tm):
    if tm < seq or seq >= GROUP_ROWS:
        return 1, tm, seq // tm
    group = max(g for g in range(1, bsz + 1) if bsz % g == 0 and g * seq <= GROUP_ROWS)
    return group, group * seq, 1


def _tile(seq, target):
    return target if seq % target == 0 else seq


def _group_layer(x2d, layer, depth, bsz, seq, wts, past, kv_bufs, final_norm):
    length = _tile(seq, 256)
    tm = _tile(bsz * seq, 512)
    if past is None:
        c0 = jnp.zeros((bsz, ML_HEADS, ML_D, ML_D), F32)
        n0 = jnp.zeros((bsz, ML_HEADS, ML_D), F32)
        m0 = jnp.zeros((bsz, ML_HEADS, LANES), F32)
        conv0 = jnp.zeros((bsz, HIST, SC_W), F32)
        ffn0 = jnp.zeros((bsz, HIST, wts["w_fconv"].shape[2]), F32)
    else:
        c0, n0 = past["C"][layer], past["n"][layer]
        m0 = jnp.broadcast_to(past["m"][layer][..., None], (bsz, ML_HEADS, LANES))
        conv0, ffn0 = past["conv"][layer], past["ffn"][layer]
    proj_args = (x2d, wts["norm1"], wts["w_head"], wts["w_tail"], wts["w_gate"], wts["w_gate_t"],
                 kv_bufs, layer, depth, tm)
    scan_args = (wts["bias_row"], wts["bias_col"], wts["ml_norm"], c0, n0, m0)
    if seq % tm == 0 and tm % length == 0:
        z, k_buf, v_buf, ya, c_new, n_new, m_new = _in_proj(*proj_args, scan=scan_args + (seq, length))
    else:
        z, k_buf, v_buf, gate, gate_t = _in_proj(*proj_args)
        gate_t = gate_t.reshape(GATE_W, bsz * seq // length, length).swapaxes(0, 1)
        ya, c_new, n_new, m_new = _mlstm(z, gate, gate_t, *scan_args, layer, bsz, seq, length)
    if past is None:
        yc = _attn_prompt(z, wts["da_lambda"], wts["da_norm"], layer, bsz, seq, 256)
    else:
        yc = _attn_sample(z, past["k"], past["v"], wts["da_lambda"], wts["da_norm"], layer, bsz, seq)
    x1, conv_new = _merge(x2d, z, ya, yc, conv0, wts["w_sc_conv"], wts["w_branch"], wts["w_out"],
                          layer, bsz, seq, _tile(seq, 512))
    x2, ffn_new = _ffn(x1, wts["norm2"], wts["norm_f"], ffn0, wts["w_fconv"], wts["w_up"], wts["w_down"],
                       layer, bsz, seq, _tile(seq, 512), final_norm)
    return x2, (k_buf, v_buf), (c_new, n_new, m_new[:, :, 0], conv_new, ffn_new)


def kernel(x_prompt, x_sample, cache_k, cache_v, state_C, state_n, state_m, state_conv, state_ffn,
           norm1, w_in, b_if, ml_norm, w_sc_conv, da_lambda, da_norm, w_branch, w_out,
           norm2, w_up, w_fconv, w_down, norm_f):
    depth, d, in_w = w_in.shape
    assert in_w == Z_W + GATE_W and d == 1024
    bp, sp, _ = x_prompt.shape
    bs, ss, _ = x_sample.shape
    past_len = cache_k.shape[2]
    g0 = 4 * HEAD_W

    w_gate = jnp.pad(w_in[:, :, g0:g0 + GATE_W], ((0, 0), (0, 0), (0, LANES - GATE_W))).astype(BF16)
    bias = jnp.concatenate([b_if[:, 0], b_if[:, 1]], axis=-1).astype(F32)
    wts = {
        "norm1": norm1.reshape(depth, 1, d),
        "w_head": w_in[:, :, :g0].astype(BF16),
        "w_tail": w_in[:, :, g0 + GATE_W:].astype(BF16),
        "w_gate": w_gate,
        "w_gate_t": jnp.swapaxes(w_gate, 1, 2)[:, :GATE_T_ROWS],
        "bias_row": jnp.pad(bias, ((0, 0), (0, LANES - GATE_W))).reshape(depth, 1, LANES),
        "bias_col": bias.reshape(depth, GATE_W, 1),
        "ml_norm": ml_norm,
        "w_sc_conv": w_sc_conv,
        "da_lambda": da_lambda,
        "da_norm": da_norm.reshape(depth, 1, DA_DV),
        "w_branch": w_branch.astype(BF16),
        "w_out": w_out.astype(BF16),
        "norm2": norm2.reshape(depth, 1, d),
        "w_up": w_up.astype(BF16),
        "w_fconv": w_fconv,
        "w_down": w_down.astype(BF16),
        "norm_f": norm_f.reshape(1, d),
    }
    past = {
        "k": cache_k.reshape(depth, bs, past_len * DA_HEADS, 2 * DA_DQK),
        "v": cache_v.reshape(depth, bs, past_len * DA_HEADS, DA_DV),
        "C": state_C, "n": state_n, "m": state_m, "conv": state_conv, "ffn": state_ffn,
    }

    xp = x_prompt.reshape(bp * sp, d)
    xs = x_sample.reshape(bs * ss, d)
    p_states, s_states = [], []
    p_kv = s_kv = None
    for layer in range(depth):
        last = layer == depth - 1
        xp, p_kv, st = _group_layer(xp, layer, depth, bp, sp, wts, None, p_kv, last)
        p_states.append(st)
        xs, s_kv, st = _group_layer(xs, layer, depth, bs, ss, wts, past, s_kv, last)
        s_states.append(st)
    kv_shape = lambda b, s: (depth, b, s, DA_HEADS, DA_DV)
    p_out = [t.reshape(kv_shape(bp, sp)) for t in p_kv] + [jnp.stack(t) for t in zip(*p_states)]
    s_out = [t.reshape(kv_shape(bs, ss)) for t in s_kv] + [jnp.stack(t) for t in zip(*s_states)]
    return (xp.reshape(bp, sp, d), xs.reshape(bs, ss, d), *p_out, *s_out)
```

```python
import functools
import itertools
import math

import jax
import jax.numpy as jnp
from jax import lax
from jax.experimental import pallas as pl
from jax.experimental.pallas import tpu as pltpu

F32 = jnp.float32
BF16 = jnp.bfloat16

EPS = 1e-6
ML_HEADS = 4
ML_D = 128
SC_W = 512
DA_HEADS = 4
DA_DQK = 64
DA_DV = 128
ATT_CHUNK = 64
ATT_HEADS_PER_STEP = 2
CONV_W = 3
HIST = CONV_W - 1
FFN_DOWN_GROUP = 4
GATE_W = 2 * ML_HEADS
GATE_T_ROWS = 16
GROUP_ROWS = 256
D_MODEL = 1024
ROW_TILE = 512
SCAN_CHUNK = 256
ATT_BLOCK = 256
MXU_W = 256
LANES = 128
HEAD_W = ML_HEADS * ML_D
Z_W = 10 * HEAD_W + 3 * D_MODEL
LN_K_SCALE = -0.5 * math.log(ML_D)
Q_SCALE = DA_DQK ** -0.5 * math.log2(math.e)
VMEM_LIMIT = 56 * 2 ** 20

Z_MQ, Z_MK, Z_MV, Z_MO, Z_SB, Z_SC, Z_SX, Z_DQ, Z_DK, Z_DV = range(10)
Z_GT = 10 * HEAD_W // D_MODEL


def _params(n_axes):
    return pltpu.CompilerParams(dimension_semantics=("arbitrary",) * n_axes,
                                vmem_limit_bytes=VMEM_LIMIT)


def _resident(block_shape, index_map):
    return pl.BlockSpec(block_shape, index_map, pipeline_mode=pl.Buffered(1))


def _rms(x, g):
    ms = jnp.mean(x * x, axis=-1, keepdims=True)
    return x * lax.rsqrt(ms + EPS) * g


def _sigmoid(x):
    return 0.5 * jnp.tanh(0.5 * x) + 0.5


def _dot(a, b):
    return jnp.dot(a, b, preferred_element_type=F32)


def _dot_nt(a, b):
    return lax.dot_general(a, b, (((1,), (1,)), ((), ())), preferred_element_type=F32)


def _dot_tn(a, b):
    return lax.dot_general(a, b, (((0,), (0,)), ((), ())), preferred_element_type=F32)


def _inproj_kernel(x_ref, g_ref, wa_ref, wb_ref, wg_ref, wgt_ref, *rest, tm, first, scan):
    rest = list(rest)
    if scan:
        brow_ref, bcol_ref, mlg_ref, c0_ref, n0_ref, m0_ref = rest[:6]
        rest = rest[6:]
    if not first:
        rest = rest[2:]
    if scan:
        z_ref, k_ref, v_ref, ya_ref, c_ref, n_ref, m_ref = rest
    else:
        z_ref, k_ref, v_ref, gate_ref, gatet_ref = rest
    h = _rms(x_ref[...], g_ref[...]).astype(BF16)

    def project(c):
        cols = slice(c * HEAD_W, (c + 1) * HEAD_W)
        w = wa_ref[:, cols] if c < Z_SB else wb_ref[:, (c - Z_SB) * HEAD_W:(c - Z_SB + 1) * HEAD_W]
        zc = _dot(h, w)
        if c == Z_DQ:
            zc = zc * Q_SCALE
        z_ref[:, cols] = zc.astype(BF16)
        if c == Z_DK:
            _store_heads(k_ref, zc, tm, first)
        if c == Z_DV:
            _store_heads(v_ref, zc, tm, first)

    scan_slabs = (Z_MQ, Z_MK, Z_MV, Z_MO)
    for c in scan_slabs:
        project(c)
    gate = _dot(h, wg_ref[...])
    gate_t = _dot_nt(wgt_ref[...], h)[:GATE_W, :]
    others = [c for c in range(Z_W // HEAD_W) if c not in scan_slabs]
    if not scan:
        for c in others:
            project(c)
        gate_ref[...] = gate
        gatet_ref[...] = gate_t
        return

    length, tiles_per_seq = scan

    @pl.when(lax.rem(pl.program_id(0), tiles_per_seq) == 0)
    def _():
        c_ref[...] = c0_ref[...]
        n_ref[...] = n0_ref[...]
        m_ref[...] = m0_ref[...]

    def chunk_stages(start):
        rows = slice(start, start + length)

        def load(slab, hd):
            return z_ref[rows, slab * HEAD_W + hd * ML_D:slab * HEAD_W + (hd + 1) * ML_D]

        def store(hd, val):
            ya_ref[rows, hd * ML_D:(hd + 1) * ML_D] = val

        return _mlstm_stages(load, store, gate[rows, :], gate_t[:, rows], brow_ref, bcol_ref, mlg_ref,
                             c_ref, n_ref, m_ref, length)

    stages = itertools.chain(*[chunk_stages(start) for start in range(0, tm, length)])
    per_chunk = -(-MLSTM_STAGES * (tm // length) // len(others))
    for c in others:
        project(c)
        for _ in itertools.islice(stages, per_chunk):
            pass
    for _ in stages:
        pass


def _store_heads(ref, slab, tm, first):
    dst = ref.at[0] if first else ref
    for h in range(DA_HEADS):
        dst[pl.ds(h, tm, stride=DA_HEADS), :] = slab[:, h * DA_DV:(h + 1) * DA_DV]
    if first:
        ref[1:] = jnp.zeros((ref.shape[0] - 1,) + ref.shape[1:], F32)


def _in_proj(x2d, norm1, w_head, w_tail, w_gate, w_gate_t, kv_bufs, layer, depth, tm, scan=None):
    n, d = x2d.shape
    first = kv_bufs is None
    row = lambda i: (i, 0)
    lay3 = lambda i: (layer, 0, 0)
    scan_specs, scan_args, scan_cfg = [], (), None
    tail_specs = [pl.BlockSpec((tm, LANES), row), pl.BlockSpec((GATE_W, tm), lambda i: (0, i))]
    tail_shapes = [jax.ShapeDtypeStruct((n, LANES), F32), jax.ShapeDtypeStruct((GATE_W, n), F32)]
    if scan is not None:
        *scan_args, seq, length = scan
        tiles = seq // tm
        bsz = n // seq
        scan_cfg = (length, tiles)
        state4 = lambda i: (i // tiles, 0, 0, 0)
        state3 = lambda i: (i // tiles, 0, 0)
        state_specs = [pl.BlockSpec((1, ML_HEADS, ML_D, ML_D), state4),
                       pl.BlockSpec((1, ML_HEADS, ML_D), state3),
                       pl.BlockSpec((1, ML_HEADS, LANES), state3)]
        scan_specs = [pl.BlockSpec((None, 1, LANES), lay3), pl.BlockSpec((None, GATE_W, 1), lay3),
                      pl.BlockSpec((None, ML_HEADS, ML_D), lay3)] + state_specs
        tail_specs = [pl.BlockSpec((tm, HEAD_W), row)] + state_specs
        tail_shapes = [jax.ShapeDtypeStruct((n, HEAD_W), BF16),
                       jax.ShapeDtypeStruct((bsz, ML_HEADS, ML_D, ML_D), F32),
                       jax.ShapeDtypeStruct((bsz, ML_HEADS, ML_D), F32),
                       jax.ShapeDtypeStruct((bsz, ML_HEADS, LANES), F32)]
    if first:
        kv_spec = pl.BlockSpec((depth, tm * DA_HEADS, DA_DV), lambda i: (0, i, 0))
        kv_specs, kv_args, aliases = [], (), {}
    else:
        kv_spec = pl.BlockSpec((None, tm * DA_HEADS, DA_DV), lambda i: (layer, i, 0))
        kv_specs = [pl.BlockSpec(memory_space=pl.ANY)] * 2
        kv_args = tuple(kv_bufs)
        first_kv = 6 + len(scan_specs)
        aliases = {first_kv: 1, first_kv + 1: 2}
    kv_shape = jax.ShapeDtypeStruct((depth, n * DA_HEADS, DA_DV), F32)
    return pl.pallas_call(
        functools.partial(_inproj_kernel, tm=tm, first=first, scan=scan_cfg),
        grid=(n // tm,),
        in_specs=[
            pl.BlockSpec((tm, d), row),
            pl.BlockSpec((None, 1, d), lay3),
            _resident((None, d, Z_SB * HEAD_W), lay3),
            _resident((None, d, Z_W - Z_SB * HEAD_W), lay3),
            _resident((None, d, LANES), lay3),
            _resident((None, GATE_T_ROWS, d), lay3),
        ] + scan_specs + kv_specs,
        out_specs=[pl.BlockSpec((tm, Z_W), row), kv_spec, kv_spec] + tail_specs,
        out_shape=[jax.ShapeDtypeStruct((n, Z_W), BF16), kv_shape, kv_shape] + tail_shapes,
        input_output_aliases=aliases,
        compiler_params=_params(1),
        name="in_proj_scan" if scan is not None else "in_proj",
    )(x2d, norm1, w_head, w_tail, w_gate, w_gate_t, *scan_args, *kv_args)


def _log_sigmoid(x):
    return jnp.minimum(x, 0.0) - jnp.log1p(jnp.exp(-jnp.abs(x)))


def _split_bf16(x):
    hi = x.astype(BF16)
    lo = (x - hi.astype(F32)).astype(BF16)
    return hi, lo


MLSTM_STAGES = 16


def _mlstm_stages(load, store, gate, gate_t, brow_ref, bcol_ref, mlg_ref, c_ref, n_ref, m_ref, L):
    rows = lax.broadcasted_iota(jnp.int32, (L, L), 0)
    cols = lax.broadcasted_iota(jnp.int32, (L, L), 1)
    tri = jnp.where(rows >= cols, 1.0, 0.0).astype(BF16)
    visible = rows <= cols

    pre_c = gate + brow_ref[...]
    pre_r = gate_t + bcol_ref[...]
    hi, lo = _split_bf16(_log_sigmoid(pre_c))
    cum_c = _dot(tri, hi) + _dot(tri, lo)
    hi, lo = _split_bf16(_log_sigmoid(pre_r))
    cum_r = _dot_nt(hi, tri) + _dot_nt(lo, tri)
    yield

    heads = range(ML_HEADS)
    q = [load(Z_MQ, h) for h in heads]
    k = [load(Z_MK, h) for h in heads]
    v = [load(Z_MV, h) for h in heads]
    b_row = [cum_r[ML_HEADS + h:ML_HEADS + h + 1, :] for h in heads]
    c_key = [pre_c[:, h:h + 1] - cum_c[:, ML_HEADS + h:ML_HEADS + h + 1] for h in heads]
    m_prev = [m_ref[0, h:h + 1, 0:1] for h in heads]
    c_prev = [c_ref[0, h] for h in heads]
    n_prev = [n_ref[0, h:h + 1, :] for h in heads]
    yield

    d = [jnp.where(visible, c_key[h], -jnp.inf) for h in heads]
    yield
    mx = [jnp.maximum(jnp.max(d[h], axis=0, keepdims=True), m_prev[h]) for h in heads]
    yield
    w = [jnp.exp(d[h] + LN_K_SCALE - mx[h]) for h in heads]
    inter = [jnp.exp(m_prev[h] - mx[h]) for h in heads]
    yield
    s = [_dot_nt(k[h], q[h]) * w[h] for h in heads]
    yield
    carried = [lax.dot_general(c_prev[h].astype(BF16), q[h], (((0,), (1,)), ((), ())),
                               preferred_element_type=F32) for h in heads]
    yield
    num = [_dot_tn(v[h], s[h].astype(BF16)) + inter[h] * carried[h] for h in heads]
    yield
    qn = [_dot_nt(n_prev[h].astype(BF16), q[h]) for h in heads]
    den = [jnp.sum(s[h], axis=0, keepdims=True) + inter[h] * qn[h] for h in heads]
    yield
    den = [jnp.maximum(jnp.abs(den[h]), jnp.exp(-(b_row[h] + mx[h]))) for h in heads]
    hid = [num[h] / den[h] for h in heads]
    yield
    hid = [hid[h] * lax.rsqrt(jnp.mean(hid[h] * hid[h], axis=0, keepdims=True) + EPS) for h in heads]
    yield
    for h in heads:
        gain = _sigmoid(load(Z_MO, h).astype(F32)) * mlg_ref[h:h + 1, :]
        store(h, (hid[h].T * gain).astype(BF16))
    yield

    mx_last = [mx[h][:, L - 1:L] for h in heads]
    il = [jnp.exp(m_prev[h] - mx_last[h]) for h in heads]
    kw = [k[h].astype(F32) * jnp.exp(c_key[h] + LN_K_SCALE - mx_last[h]) for h in heads]
    yield
    kv = [_dot_tn(kw[h].astype(BF16), v[h]) for h in heads]
    yield
    for h in heads:
        c_ref[0, h] = il[h] * c_prev[h] + kv[h]
        n_ref[0, h:h + 1, :] = il[h] * n_prev[h] + jnp.sum(kw[h], axis=0, keepdims=True)
    yield
    for h in heads:
        m_ref[0, h:h + 1, :] = jnp.broadcast_to(b_row[h][:, L - 1:L] + mx_last[h], (1, LANES))
    yield


def _mlstm_kernel(q_ref, k_ref, v_ref, o_ref, gate_ref, gatet_ref, brow_ref, bcol_ref, mlg_ref,
                  c0_ref, n0_ref, m0_ref, ya_ref, c_ref, n_ref, m_ref, *, length):
    @pl.when(pl.program_id(1) == 0)
    def _():
        c_ref[...] = c0_ref[...]
        n_ref[...] = n0_ref[...]
        m_ref[...] = m0_ref[...]

    slabs = {Z_MQ: q_ref, Z_MK: k_ref, Z_MV: v_ref, Z_MO: o_ref}

    def load(slab, h):
        return slabs[slab][:, h * ML_D:(h + 1) * ML_D]

    def store(h, val):
        ya_ref[:, h * ML_D:(h + 1) * ML_D] = val

    for _ in _mlstm_stages(load, store, gate_ref[...], gatet_ref[...], brow_ref, bcol_ref, mlg_ref,
                           c_ref, n_ref, m_ref, length):
        pass


def _mlstm(z, gate, gate_t, bias_row, bias_col, ml_g, c0, n0, m0, layer, bsz, seq, length):
    n = bsz * seq
    nc = seq // length
    row = lambda col: (lambda b, j: (b * nc + j, col))
    lay3 = lambda b, j: (layer, 0, 0)
    state4 = lambda b, j: (b, 0, 0, 0)
    state3 = lambda b, j: (b, 0, 0)
    return pl.pallas_call(
        functools.partial(_mlstm_kernel, length=length),
        grid=(bsz, nc),
        in_specs=[
            pl.BlockSpec((length, HEAD_W), row(Z_MQ)),
            pl.BlockSpec((length, HEAD_W), row(Z_MK)),
            pl.BlockSpec((length, HEAD_W), row(Z_MV)),
            pl.BlockSpec((length, HEAD_W), row(Z_MO)),
            pl.BlockSpec((length, LANES), row(0)),
            pl.BlockSpec((None, GATE_W, length), lambda b, j: (b * nc + j, 0, 0)),
            pl.BlockSpec((None, 1, LANES), lay3),
            pl.BlockSpec((None, GATE_W, 1), lay3),
            pl.BlockSpec((None, ML_HEADS, ML_D), lay3),
            pl.BlockSpec((1, ML_HEADS, ML_D, ML_D), state4),
            pl.BlockSpec((1, ML_HEADS, ML_D), state3),
            pl.BlockSpec((1, ML_HEADS, LANES), state3),
        ],
        out_specs=[
            pl.BlockSpec((length, HEAD_W), row(0)),
            pl.BlockSpec((1, ML_HEADS, ML_D, ML_D), state4),
            pl.BlockSpec((1, ML_HEADS, ML_D), state3),
            pl.BlockSpec((1, ML_HEADS, LANES), state3),
        ],
        out_shape=[
            jax.ShapeDtypeStruct((n, HEAD_W), BF16),
            jax.ShapeDtypeStruct((bsz, ML_HEADS, ML_D, ML_D), F32),
            jax.ShapeDtypeStruct((bsz, ML_HEADS, ML_D), F32),
            jax.ShapeDtypeStruct((bsz, ML_HEADS, LANES), F32),
        ],
        compiler_params=_params(2),
        name="mlstm",
    )(z, z, z, z, gate, gate_t, bias_row, bias_col, ml_g, c0, n0, m0)


def _lambda(lam_ref, lam_init):
    lp = lam_ref[...]
    a = jnp.sum(lp[0:1] * lp[1:2], axis=1, keepdims=True)
    b = jnp.sum(lp[2:3] * lp[3:4], axis=1, keepdims=True)
    return jnp.exp(a) - jnp.exp(b) + lam_init


def _split_q(q):
    lane = lax.broadcasted_iota(jnp.int32, q.shape, 1)
    zero = jnp.zeros_like(q)
    return jnp.where(lane < DA_DQK, q, zero), jnp.where(lane >= DA_DQK, q, zero)


def _attn_prompt_kernel(lam_ref, dag_ref, q_ref, k_ref, v_ref, o_ref, *, blk, nq, heads, lam_init):
    lam = _lambda(lam_ref, lam_init)
    rows = lax.broadcasted_iota(jnp.int32, (2 * blk, blk), 0)
    cols = lax.broadcasted_iota(jnp.int32, (2 * blk, blk), 1)
    mask = (cols // ATT_CHUNK) <= ((rows % blk) // ATT_CHUNK)
    half = blk // 2
    units = [(h, c) for h in range(heads) for c in range(nq)]
    st = [dict(s=[], p=[]) for _ in units]
    keys = lambda j: slice(j * blk, (j + 1) * blk)
    lanes = lambda h: slice(h * DA_DV, (h + 1) * DA_DV)

    def lane_fold(x, op):
        return op(x[:, :half], x[:, half:])

    def pass1(u):
        b = st[u]
        h, c = units[u]

        def start():
            b["q2"] = jnp.concatenate(_split_q(q_ref[keys(c), lanes(h)]), axis=0)

        def chunk(j):
            s = _dot_nt(b["q2"], k_ref[keys(j), lanes(h)])
            if j == c:
                s = jnp.where(mask, s, -jnp.inf)
            mj = lane_fold(s, jnp.maximum)
            b["m"] = mj if j == 0 else jnp.maximum(b["m"], mj)
            b["s"].append(s)

        def finish():
            b["m"] = jnp.max(b["m"], axis=1, keepdims=True)

        return [start] + [functools.partial(chunk, j) for j in range(c + 1)] + [finish]

    def pass2(u):
        b = st[u]
        c = units[u][1]

        def chunk(j):
            p = jnp.exp2(b["s"][j] - b["m"])
            lj = lane_fold(p, jnp.add)
            b["l"] = lj if j == 0 else b["l"] + lj
            b["p"].append(p)

        def finish():
            l = jnp.sum(b["l"], axis=1, keepdims=True)
            b["l0"] = l[:blk]
            b["rho"] = lam * l[:blk] / l[blk:]

        return [functools.partial(chunk, j) for j in range(c + 1)] + [finish]

    def pass3(u):
        b = st[u]
        h, c = units[u]

        def chunk(j):
            p = b["p"][j]
            part = _dot((p[:blk] - b["rho"] * p[blk:]).astype(BF16), v_ref[keys(j), lanes(h)])
            b["o"] = part if j == 0 else b["o"] + part

        def finish():
            o = b["o"] / b["l0"]
            o_ref[keys(c), lanes(h)] = (_rms(o, dag_ref[...]) * (1.0 - lam_init)).astype(BF16)
            b.clear()

        return [functools.partial(chunk, j) for j in range(c + 1)] + [finish]

    for stage in range(len(units) + 2):
        active = [make(stage - lag) for lag, make in enumerate((pass1, pass2, pass3))
                  if 0 <= stage - lag < len(units)]
        for group in itertools.zip_longest(*active):
            for thunk in group:
                if thunk is not None:
                    thunk()


def _attn_prompt(z, da_lambda, da_norm, layer, bsz, seq, blk):
    nq = seq // blk
    lam_init = 0.8 - 0.6 * math.exp(-0.3 * layer)
    lay3 = lambda b, h: (layer, 0, 0)
    hps = ATT_HEADS_PER_STEP
    whole = lambda slab: pl.BlockSpec((seq, hps * DA_DV), lambda b, h: (b, slab * (DA_HEADS // hps) + h))
    return pl.pallas_call(
        functools.partial(_attn_prompt_kernel, blk=blk, nq=nq, heads=hps, lam_init=lam_init),
        grid=(bsz, DA_HEADS // hps),
        in_specs=[
            pl.BlockSpec((None, 4, DA_DQK), lay3),
            pl.BlockSpec((None, 1, DA_DV), lay3),
            whole(Z_DQ), whole(Z_DK), whole(Z_DV),
        ],
        out_specs=pl.BlockSpec((seq, hps * DA_DV), lambda b, h: (b, h)),
        out_shape=jax.ShapeDtypeStruct((bsz * seq, HEAD_W), BF16),
        compiler_params=_params(2),
        name="attn_prompt",
    )(da_lambda, da_norm, z, z, z)


def _attn_sample_kernel(lam_ref, dag_ref, q_ref, kn_ref, vn_ref, kp_ref, vp_ref, o_ref, *, past, lam_init):
    lam = _lambda(lam_ref, lam_init)
    t = q_ref.shape[0]
    for h in range(DA_HEADS):
        hs = slice(h * DA_DV, (h + 1) * DA_DV)
        q2 = jnp.concatenate(_split_q(q_ref[:, hs]), axis=0)
        kp = kp_ref[pl.ds(h, past, stride=DA_HEADS), :].astype(BF16)
        vp = vp_ref[pl.ds(h, past, stride=DA_HEADS), :].astype(BF16)
        sp = _dot_nt(q2, kp)
        sn = _dot_nt(q2, kn_ref[:, hs])
        m = jnp.maximum(jnp.max(sp, axis=1, keepdims=True), jnp.max(sn, axis=1, keepdims=True))
        pp = jnp.exp2(sp - m)
        pn = jnp.exp2(sn - m)
        l = jnp.sum(pp, axis=1, keepdims=True) + jnp.sum(pn, axis=1, keepdims=True)
        rho = lam * l[:t] / l[t:]
        o = (_dot((pp[:t] - rho * pp[t:]).astype(BF16), vp)
             + _dot((pn[:t] - rho * pn[t:]).astype(BF16), vn_ref[:, hs]))
        o = o / l[:t]
        o_ref[:, hs] = (_rms(o, dag_ref[...]) * (1.0 - lam_init)).astype(BF16)


def _attn_sample(z, cache_k, cache_v, da_lambda, da_norm, layer, bsz, seq):
    past = cache_k.shape[2] // DA_HEADS
    lam_init = 0.8 - 0.6 * math.exp(-0.3 * layer)
    lay3 = lambda b: (layer, 0, 0)
    cache = pl.BlockSpec((None, None, past * DA_HEADS, DA_DV), lambda b: (layer, b, 0, 0))
    slab = lambda col: pl.BlockSpec((seq, HEAD_W), lambda b: (b, col))
    return pl.pallas_call(
        functools.partial(_attn_sample_kernel, past=past, lam_init=lam_init),
        grid=(bsz,),
        in_specs=[
            pl.BlockSpec((None, 4, DA_DQK), lay3),
            pl.BlockSpec((None, 1, DA_DV), lay3),
            slab(Z_DQ), slab(Z_DK), slab(Z_DV),
            cache, cache,
        ],
        out_specs=pl.BlockSpec((seq, HEAD_W), lambda b: (b, 0)),
        out_shape=jax.ShapeDtypeStruct((bsz * seq, HEAD_W), BF16),
        compiler_params=_params(1),
        name="attn_sample",
    )(da_lambda, da_norm, z, z, z, cache_k, cache_v)


def _causal_conv3(u, hist, w):
    row = lax.broadcasted_iota(jnp.int32, (8, u.shape[1]), 0)
    r1 = pltpu.roll(u, 1, 0)
    r2 = pltpu.roll(u, 2, 0)
    top1 = jnp.where(row == 0, hist[1:2], r1[:8])
    top2 = jnp.where(row == 0, hist[0:1], jnp.where(row == 1, hist[1:2], r2[:8]))
    x1 = jnp.concatenate([top1, r1[8:]], axis=0)
    x2 = jnp.concatenate([top2, r2[8:]], axis=0)
    return w[2:3] * u + w[1:2] * x1 + w[0:1] * x2


def _init_history(hbuf, hist_ref):
    @pl.when(pl.program_id(1) == 0)
    def _():
        hbuf[...] = hist_ref[...]


def _conv_sequences(u, hbuf, cols, w):
    group = hbuf.shape[0]
    rows_per = u.shape[0] // group
    out = []
    for g in range(group):
        ug = u[g * rows_per:(g + 1) * rows_per, :]
        out.append(_causal_conv3(ug, hbuf[g, :, cols], w))
        hbuf[g, :, cols] = ug[rows_per - HIST:rows_per, :]
    return out[0] if group == 1 else jnp.concatenate(out, axis=0)


def _merge_kernel(x_ref, sb_ref, sc_ref, sx_ref, g0_ref, g1_ref, g2_ref, ya_ref, yc_ref, hist_ref,
                  scw_ref, wbr_ref, wo_ref, x1_ref, cs_ref, hbuf, *, tm):
    _init_history(hbuf, hist_ref)
    u = sc_ref[...].astype(F32) * sx_ref[...].astype(F32)
    cu = _conv_sequences(u, hbuf, slice(None), scw_ref[...])
    cs_ref[...] = hbuf[...]
    yb = (sb_ref[...].astype(F32) * cu).astype(BF16)
    mix = (_sigmoid(g0_ref[...].astype(F32)) * _dot(ya_ref[...], wbr_ref[0])
           + _sigmoid(g1_ref[...].astype(F32)) * _dot(yb, wbr_ref[1])
           + _sigmoid(g2_ref[...].astype(F32)) * _dot(yc_ref[...], wbr_ref[2]))
    x1_ref[...] = x_ref[...] + _dot(mix.astype(BF16), wo_ref[...])


def _merge(x2d, z, ya, yc, hist, sc_w, w_br, w_o, layer, bsz, seq, tm):
    n, d = x2d.shape
    group, tm, nt = _tiling(bsz, seq, tm)
    row = lambda col: (lambda b, j: (b * nt + j, col))
    lay3 = lambda b, j: (layer, 0, 0)
    state = lambda b, j: (b, 0, 0)
    return pl.pallas_call(
        functools.partial(_merge_kernel, tm=tm),
        grid=(bsz // group, nt),
        in_specs=[
            pl.BlockSpec((tm, d), row(0)),
            pl.BlockSpec((tm, HEAD_W), row(Z_SB)),
            pl.BlockSpec((tm, HEAD_W), row(Z_SC)),
            pl.BlockSpec((tm, HEAD_W), row(Z_SX)),
            pl.BlockSpec((tm, d), row(Z_GT)),
            pl.BlockSpec((tm, d), row(Z_GT + 1)),
            pl.BlockSpec((tm, d), row(Z_GT + 2)),
            pl.BlockSpec((tm, HEAD_W), row(0)),
            pl.BlockSpec((tm, HEAD_W), row(0)),
            pl.BlockSpec((group, HIST, SC_W), state),
            pl.BlockSpec((None, CONV_W, SC_W), lay3),
            _resident((None, 3, HEAD_W, d), lambda b, j: (layer, 0, 0, 0)),
            _resident((None, d, d), lay3),
        ],
        out_specs=[
            pl.BlockSpec((tm, d), row(0)),
            pl.BlockSpec((group, HIST, SC_W), state),
        ],
        out_shape=[
            jax.ShapeDtypeStruct((n, d), F32),
            jax.ShapeDtypeStruct((bsz, HIST, SC_W), F32),
        ],
        scratch_shapes=[pltpu.VMEM((group, HIST, SC_W), F32)],
        compiler_params=_params(2),
        name="merge",
    )(x2d, z, z, z, z, z, z, ya, yc, hist, sc_w, w_br, w_o)


def _ffn_kernel(x_ref, g2_ref, gf_ref, hist_ref, fw_ref, wup_ref, wdn_ref, y_ref, fs_ref, hbuf,
                *, tm, d_ff, chunk, group, final_norm):
    _init_history(hbuf, hist_ref)
    x = x_ref[...]
    h2 = _rms(x, g2_ref[...]).astype(BF16)
    n_chunks = d_ff // chunk

    def up_proj(i):
        lo = slice(i * chunk, (i + 1) * chunk)
        hi = slice(d_ff + i * chunk, d_ff + (i + 1) * chunk)
        return (lo, _dot(h2, wup_ref[:, lo])), (hi, _dot(h2, wup_ref[:, hi]))

    def conv(cols, up):
        return _conv_sequences(up, hbuf, cols, fw_ref[:, cols])

    ahead = up_proj(0)
    acc = None
    pieces = []
    for i in range(n_chunks):
        (lo, ua), (hi, ub) = ahead
        if i + 1 < n_chunks:
            ahead = up_proj(i + 1)
        a = conv(lo, ua)
        b = conv(hi, ub)
        pieces.append((a * _sigmoid(a) * b).astype(BF16))
        if len(pieces) == group or i + 1 == n_chunks:
            k0 = (i + 1 - len(pieces)) * chunk
            part = _dot(jnp.concatenate(pieces, axis=1), wdn_ref[k0:(i + 1) * chunk, :])
            acc = part if acc is None else acc + part
            pieces = []
    fs_ref[...] = hbuf[...]
    y = x + acc
    if final_norm:
        y = _rms(y, gf_ref[...])
    y_ref[...] = y


def _ffn(x2d, norm2, norm_f, hist, w_fconv, w_up, w_down, layer, bsz, seq, tm, final_norm):
    n, d = x2d.shape
    d_ff = w_down.shape[1]
    group, tm, nt = _tiling(bsz, seq, tm)
    chunk = MXU_W if d_ff % MXU_W == 0 else LANES
    row = lambda b, j: (b * nt + j, 0)
    lay3 = lambda b, j: (layer, 0, 0)
    state = lambda b, j: (b, 0, 0)
    return pl.pallas_call(
        functools.partial(_ffn_kernel, tm=tm, d_ff=d_ff, chunk=chunk, group=FFN_DOWN_GROUP,
                          final_norm=final_norm),
        grid=(bsz // group, nt),
        in_specs=[
            pl.BlockSpec((tm, d), row),
            pl.BlockSpec((None, 1, d), lay3),
            pl.BlockSpec((1, d), lambda b, j: (0, 0)),
            pl.BlockSpec((group, HIST, 2 * d_ff), state),
            pl.BlockSpec((None, CONV_W, 2 * d_ff), lay3),
            _resident((None, d, 2 * d_ff), lay3),
            _resident((None, d_ff, d), lay3),
        ],
        out_specs=[
            pl.BlockSpec((tm, d), row),
            pl.BlockSpec((group, HIST, 2 * d_ff), state),
        ],
        out_shape=[
            jax.ShapeDtypeStruct((n, d), F32),
            jax.ShapeDtypeStruct((bsz, HIST, 2 * d_ff), F32),
        ],
        scratch_shapes=[pltpu.VMEM((group, HIST, 2 * d_ff), F32)],
        compiler_params=_params(2),
        name="ffn",
    )(x2d, norm2, norm_f, hist, w_fconv, w_up, w_down)


def _tiling(bsz, seq, tm):
    if tm < seq or seq >= GROUP_ROWS:
        return 1, tm, seq // tm
    group = max(g for g in range(1, bsz + 1) if bsz % g == 0 and g * seq <= GROUP_ROWS)
    return group, group * seq, 1


def _tile(seq, target):
    return target if seq % target == 0 else seq


def _group_layer(x2d, layer, depth, bsz, seq, wts, past, kv_bufs, final_norm):
    length = _tile(seq, SCAN_CHUNK)
    tm = _tile(bsz * seq, ROW_TILE)
    if past is None:
        c0 = jnp.zeros((bsz, ML_HEADS, ML_D, ML_D), F32)
        n0 = jnp.zeros((bsz, ML_HEADS, ML_D), F32)
        m0 = jnp.zeros((bsz, ML_HEADS, LANES), F32)
        conv0 = jnp.zeros((bsz, HIST, SC_W), F32)
        ffn0 = jnp.zeros((bsz, HIST, wts["w_fconv"].shape[2]), F32)
    else:
        c0, n0 = past["C"][layer], past["n"][layer]
        m0 = jnp.broadcast_to(past["m"][layer][..., None], (bsz, ML_HEADS, LANES))
        conv0, ffn0 = past["conv"][layer], past["ffn"][layer]
    proj_args = (x2d, wts["norm1"], wts["w_head"], wts["w_tail"], wts["w_gate"], wts["w_gate_t"],
                 kv_bufs, layer, depth, tm)
    scan_args = (wts["bias_row"], wts["bias_col"], wts["ml_norm"], c0, n0, m0)
    if seq % tm == 0 and tm % length == 0:
        z, k_buf, v_buf, ya, c_new, n_new, m_new = _in_proj(*proj_args, scan=scan_args + (seq, length))
    else:
        z, k_buf, v_buf, gate, gate_t = _in_proj(*proj_args)
        gate_t = gate_t.reshape(GATE_W, bsz * seq // length, length).swapaxes(0, 1)
        ya, c_new, n_new, m_new = _mlstm(z, gate, gate_t, *scan_args, layer, bsz, seq, length)
    if past is None:
        yc = _attn_prompt(z, wts["da_lambda"], wts["da_norm"], layer, bsz, seq, ATT_BLOCK)
    else:
        yc = _attn_sample(z, past["k"], past["v"], wts["da_lambda"], wts["da_norm"], layer, bsz, seq)
    x1, conv_new = _merge(x2d, z, ya, yc, conv0, wts["w_sc_conv"], wts["w_branch"], wts["w_out"],
                          layer, bsz, seq, _tile(seq, ROW_TILE))
    x2, ffn_new = _ffn(x1, wts["norm2"], wts["norm_f"], ffn0, wts["w_fconv"], wts["w_up"], wts["w_down"],
                       layer, bsz, seq, _tile(seq, ROW_TILE), final_norm)
    return x2, (k_buf, v_buf), (c_new, n_new, m_new[:, :, 0], conv_new, ffn_new)


def kernel(x_prompt, x_sample, cache_k, cache_v, state_C, state_n, state_m, state_conv, state_ffn,
           norm1, w_in, b_if, ml_norm, w_sc_conv, da_lambda, da_norm, w_branch, w_out,
           norm2, w_up, w_fconv, w_down, norm_f):
    depth, d, in_w = w_in.shape
    assert in_w == Z_W + GATE_W and d == D_MODEL
    bp, sp, _ = x_prompt.shape
    bs, ss, _ = x_sample.shape
    past_len = cache_k.shape[2]
    g0 = 4 * HEAD_W

    w_gate = jnp.pad(w_in[:, :, g0:g0 + GATE_W], ((0, 0), (0, 0), (0, LANES - GATE_W))).astype(BF16)
    bias = jnp.concatenate([b_if[:, 0], b_if[:, 1]], axis=-1).astype(F32)
    wts = {
        "norm1": norm1.reshape(depth, 1, d),
        "w_head": w_in[:, :, :g0].astype(BF16),
        "w_tail": w_in[:, :, g0 + GATE_W:].astype(BF16),
        "w_gate": w_gate,
        "w_gate_t": jnp.swapaxes(w_gate, 1, 2)[:, :GATE_T_ROWS],
        "bias_row": jnp.pad(bias, ((0, 0), (0, LANES - GATE_W))).reshape(depth, 1, LANES),
        "bias_col": bias.reshape(depth, GATE_W, 1),
        "ml_norm": ml_norm,
        "w_sc_conv": w_sc_conv,
        "da_lambda": da_lambda,
        "da_norm": da_norm.reshape(depth, 1, DA_DV),
        "w_branch": w_branch.astype(BF16),
        "w_out": w_out.astype(BF16),
        "norm2": norm2.reshape(depth, 1, d),
        "w_up": w_up.astype(BF16),
        "w_fconv": w_fconv,
        "w_down": w_down.astype(BF16),
        "norm_f": norm_f.reshape(1, d),
    }
    past = {
        "k": cache_k.reshape(depth, bs, past_len * DA_HEADS, 2 * DA_DQK),
        "v": cache_v.reshape(depth, bs, past_len * DA_HEADS, DA_DV),
        "C": state_C, "n": state_n, "m": state_m, "conv": state_conv, "ffn": state_ffn,
    }

    xp = x_prompt.reshape(bp * sp, d)
    xs = x_sample.reshape(bs * ss, d)
    p_states, s_states = [], []
    p_kv = s_kv = None
    for layer in range(depth):
        last = layer == depth - 1
        xp, p_kv, st = _group_layer(xp, layer, depth, bp, sp, wts, None, p_kv, last)
        p_states.append(st)
        xs, s_kv, st = _group_layer(xs, layer, depth, bs, ss, wts, past, s_kv, last)
        s_states.append(st)
    kv_shape = lambda b, s: (depth, b, s, DA_HEADS, DA_DV)
    p_out = [t.reshape(kv_shape(bp, sp)) for t in p_kv] + [jnp.stack(t) for t in zip(*p_states)]
    s_out = [t.reshape(kv_shape(bs, ss)) for t in s_kv] + [jnp.stack(t) for t in zip(*s_states)]
    return (xp.reshape(bp, sp, d), xs.reshape(bs, ss, d), *p_out, *s_out)
```
